```python
import math
import jax, jax.numpy as jnp
from jax import lax
import numpy as np

D_MODEL = 1024
BATCH = 8
SEQ = 8192
DEPTH = 1

N_META = 16
BLOCK = 128
PAD = BLOCK - N_META

HG_HEADS = 4
HG_K = 128
HG_V = 128
HG_KW = HG_HEADS * HG_K
HG_VW = HG_HEADS * HG_V
SUB = 16

ATT_HEADS = 8
ATT_KV_HEADS = 2
HEAD_DIM = 64
ATT_QW = ATT_HEADS * HEAD_DIM
ATT_KVW = ATT_KV_HEADS * HEAD_DIM
WINDOW = 128
ROPE_THETA = 10000.0

N_BRANCH = 2
D_FF = ((-(-8 * D_MODEL // 3) + 255) // 256) * 256
EPS = 1e-5
ALPHA = (2.0 * DEPTH) ** 0.25
BETA = (8.0 * DEPTH) ** -0.25
SPLIT_SIZES = (HG_KW, HG_KW, HG_VW, HG_VW, ATT_QW, ATT_KVW, ATT_KVW, N_BRANCH * D_MODEL)
IN_W = sum(SPLIT_SIZES)

kernel_name = "hgrn2_swa_sink_hybrid_deepnorm"


def layer_norm(x, g, b):
    xf = x.astype(jnp.float32)
    mu = jnp.mean(xf, axis=-1, keepdims=True)
    var = jnp.mean(jnp.square(xf - mu), axis=-1, keepdims=True)
    y = (xf - mu) * lax.rsqrt(var + EPS) * g.astype(jnp.float32) + b.astype(jnp.float32)
    return y.astype(x.dtype)


def rope(x, pos):
    half = HEAD_DIM // 2
    inv = ROPE_THETA ** (-jnp.arange(half, dtype=jnp.float32) / half)
    ang = pos.astype(jnp.float32)[:, None] * inv[None, :]
    cos = jnp.cos(ang)[None, :, None, :]
    sin = jnp.sin(ang)[None, :, None, :]
    xf = x.astype(jnp.float32)
    x1, x2 = xf[..., :half], xf[..., half:]
    return jnp.concatenate([x1 * cos - x2 * sin, x2 * cos + x1 * sin], axis=-1).astype(x.dtype)


def hgrn2_chunk(state, inp):
    q, k, v, log_f = inp
    B, H, C, K = q.shape
    V = v.shape[-1]
    n = C // SUB
    b = jnp.cumsum(log_f, axis=2)
    b_last = b[:, :, -1]
    o_inter = jnp.einsum('bhck,bhkv->bhcv', q * jnp.exp(b), state)
    qs = q.reshape(B, H, n, SUB, K)
    ks = k.reshape(B, H, n, SUB, K)
    vs = v.reshape(B, H, n, SUB, V)
    bs = b.reshape(B, H, n, SUB, K)
    tri = jnp.tril(jnp.ones((SUB, SUB), dtype=bool))[:, :, None]
    diff = bs[:, :, :, :, None, :] - bs[:, :, :, None, :, :]
    decay = jnp.exp(jnp.where(tri, diff, -jnp.inf))
    a_diag = jnp.einsum('bhntk,bhnsk,bhntsk->bhnts', qs, ks, decay)
    o_diag = jnp.einsum('bhnts,bhnsv->bhntv', a_diag, vs)
    b_ref = jnp.concatenate([jnp.zeros_like(bs[:, :, :1, 0]), bs[:, :, :-1, -1]], axis=2)
    q_off = qs * jnp.exp(bs - b_ref[:, :, :, None])
    earlier = (jnp.arange(C) // SUB)[None, :] < jnp.arange(n)[:, None]
    k_off = k[:, :, None] * jnp.exp(jnp.where(earlier[:, :, None],
                                              b_ref[:, :, :, None] - b[:, :, None], -jnp.inf))
    a_off = jnp.einsum('bhntk,bhnsk->bhnts', q_off, k_off)
    o_off = jnp.einsum('bhnts,bhsv->bhntv', a_off, v)
    o = o_inter + (o_diag + o_off).reshape(B, H, C, V)
    new_state = state * jnp.exp(b_last)[..., None] + jnp.einsum(
        'bhck,bhcv->bhkv', k * jnp.exp(b_last[:, :, None] - b), v)
    return new_state, o


def hgrn2_mixer(q_raw, f_raw, i_raw, g_raw, lower_bound, norm_g, valid):
    B, P, _ = q_raw.shape
    N = P // BLOCK
    f32 = jnp.float32
    q = jax.nn.silu(q_raw.astype(f32))
    fg = lower_bound + (1.0 - lower_bound) * jax.nn.sigmoid(f_raw.astype(f32))
    m = valid[None, :, None]
    log_f = jnp.where(m, jnp.log(fg), 0.0)
    k = jnp.where(m, 1.0 - fg, 0.0)
    v = i_raw.astype(f32)

    def to_chunks(t, dh):
        return t.reshape(B, N, BLOCK, HG_HEADS, dh).transpose(1, 0, 3, 2, 4)

    xs = (to_chunks(q, HG_K), to_chunks(k, HG_K), to_chunks(v, HG_V), to_chunks(log_f, HG_K))
    s0 = jnp.zeros((B, HG_HEADS, HG_K, HG_V), f32)
    _, o = lax.scan(hgrn2_chunk, s0, xs)
    o = o.transpose(1, 0, 3, 2, 4).reshape(B, P, HG_HEADS, HG_V)
    o = o * lax.rsqrt(jnp.mean(o * o, axis=-1, keepdims=True) + EPS) * norm_g.astype(f32)
    gate = jax.nn.silu(g_raw.astype(f32)).reshape(B, P, HG_HEADS, HG_V)
    return (o * gate).reshape(B, P, HG_VW).astype(q_raw.dtype)


def swa_sink_attention(q_raw, k_raw, v_raw, sinks, pos):
    B, P, _ = q_raw.shape
    NB = P // BLOCK
    G = ATT_HEADS // ATT_KV_HEADS
    f32 = jnp.float32
    q = rope(q_raw.reshape(B, P, ATT_HEADS, HEAD_DIM), pos).astype(f32)
    k = rope(k_raw.reshape(B, P, ATT_KV_HEADS, HEAD_DIM), pos).astype(f32)
    v = v_raw.reshape(B, P, ATT_KV_HEADS, HEAD_DIM).astype(f32)
    scale = HEAD_DIM ** -0.5
    qb = q.reshape(B, NB, BLOCK, ATT_KV_HEADS, G, HEAD_DIM)
    kb = k.reshape(B, NB, BLOCK, ATT_KV_HEADS, HEAD_DIM)
    vb = v.reshape(B, NB, BLOCK, ATT_KV_HEADS, HEAD_DIM)
    shift = lambda t: jnp.pad(t, ((0, 0), (1, 0), (0, 0), (0, 0), (0, 0)))[:, :-1]
    k_band = jnp.concatenate([shift(kb), kb], axis=2)
    v_band = jnp.concatenate([shift(vb), vb], axis=2)
    k_meta = k[:, PAD:BLOCK]
    v_meta = v[:, PAD:BLOCK]
    pos_b = pos.reshape(NB, BLOCK)
    pos_prev = jnp.concatenate([jnp.full((1, BLOCK), -1, pos.dtype), pos_b[:-1]], axis=0)
    key_pos = jnp.concatenate([pos_prev, pos_b], axis=1)[:, None, :]
    meta_pos = pos[PAD:BLOCK][None, None, :]
    qp = pos_b[:, :, None]
    band_ok = (key_pos >= N_META) & (key_pos <= qp) & (qp - key_pos < WINDOW)
    meta_ok = meta_pos <= qp
    neg = jnp.finfo(f32).min
    s_band = jnp.einsum('bnqhgd,bnkhd->bnhgqk', qb, k_band) * scale
    s_meta = jnp.einsum('bnqhgd,bmhd->bnhgqm', qb, k_meta) * scale
    s_band = jnp.where(band_ok[None, :, None, None], s_band, neg)
    s_meta = jnp.where(meta_ok[None, :, None, None], s_meta, neg)
    sink = jnp.broadcast_to(sinks.astype(f32).reshape(ATT_KV_HEADS, G)[None, None, :, :, None, None],
                            s_meta.shape[:-1] + (1,))
    p = jax.nn.softmax(jnp.concatenate([s_meta, s_band, sink], axis=-1), axis=-1)
    p_meta = p[..., :N_META]
    p_band = p[..., N_META:N_META + 2 * BLOCK]
    o = (jnp.einsum('bnhgqm,bmhd->bnqhgd', p_meta, v_meta)
         + jnp.einsum('bnhgqk,bnkhd->bnqhgd', p_band, v_band))
    return o.reshape(B, P, ATT_QW).astype(q_raw.dtype)


def setup_inputs(seed: int = 0) -> dict:
    key = jax.random.key(seed)
    ks = jax.random.split(key, 18)
    f32 = jnp.float32
    nrm = lambda k, shape, s: jax.random.normal(k, shape, f32) * s
    return {
        "x": nrm(ks[0], (BATCH, SEQ, D_MODEL), 1.0),
        "meta_tokens": nrm(ks[1], (N_META, D_MODEL), 1.0),
        "ln_emb_g": 1.0 + nrm(ks[2], (D_MODEL,), 0.02),
        "ln_emb_b": nrm(ks[3], (D_MODEL,), 0.02),
        "w_in": nrm(ks[4], (DEPTH, D_MODEL, IN_W), D_MODEL ** -0.5),
        "hg_lower_bounds": nrm(ks[5], (DEPTH + 1, HG_KW), 0.1),
        "hg_norm_g": 1.0 + nrm(ks[6], (DEPTH, HG_V), 0.02),
        "attn_sinks": nrm(ks[7], (DEPTH, ATT_HEADS), 0.5),
        "w_branch_hg": nrm(ks[8], (DEPTH, HG_VW, D_MODEL), HG_VW ** -0.5),
        "w_branch_attn": nrm(ks[9], (DEPTH, ATT_QW, D_MODEL), ATT_QW ** -0.5),
        "w_out": nrm(ks[10], (DEPTH, D_MODEL, D_MODEL), BETA * D_MODEL ** -0.5),
        "ln1_g": 1.0 + nrm(ks[11], (DEPTH, D_MODEL), 0.02),
        "ln1_b": nrm(ks[12], (DEPTH, D_MODEL), 0.02),
        "w_ffn_in": nrm(ks[13], (DEPTH, D_MODEL, 2 * D_FF), D_MODEL ** -0.5),
        "w_ffn_out": nrm(ks[14], (DEPTH, D_FF, D_MODEL), BETA * D_FF ** -0.5),
        "ln2_g": 1.0 + nrm(ks[15], (DEPTH, D_MODEL), 0.02),
        "ln2_b": nrm(ks[16], (DEPTH, D_MODEL), 0.02),
    }


def reference(x, meta_tokens, ln_emb_g, ln_emb_b, w_in, hg_lower_bounds, hg_norm_g, attn_sinks,
              w_branch_hg, w_branch_attn, w_out, ln1_g, ln1_b, w_ffn_in, w_ffn_out, ln2_g, ln2_b):
    B, S, D = x.shape
    P = S + BLOCK
    meta = jnp.broadcast_to(meta_tokens.astype(x.dtype)[None], (B, N_META, D))
    h = layer_norm(jnp.concatenate([meta, x], axis=1), ln_emb_g, ln_emb_b)
    h = jnp.pad(h, ((0, 0), (PAD, 0), (0, 0)))
    pos = jnp.arange(P, dtype=jnp.int32) - PAD
    valid = pos >= 0
    lbs = jnp.cumsum(jax.nn.softmax(hg_lower_bounds.astype(jnp.float32), axis=0), axis=0)
    split_idx = [sum(SPLIT_SIZES[:i + 1]) for i in range(len(SPLIT_SIZES) - 1)]
    for l in range(DEPTH):
        proj = h @ w_in[l]
        hq, hf, hi, hg, aq, ak, av, gates = jnp.split(proj, split_idx, axis=-1)
        y_hg = hgrn2_mixer(hq, hf, hi, hg, lbs[l], hg_norm_g[l], valid) @ w_branch_hg[l]
        y_att = swa_sink_attention(aq, ak, av, attn_sinks[l], pos) @ w_branch_attn[l]
        g_hg, g_att = jnp.split(jax.nn.sigmoid(gates), N_BRANCH, axis=-1)
        mix = (g_hg * y_hg + g_att * y_att) @ w_out[l]
        h = layer_norm(ALPHA * h + mix, ln1_g[l], ln1_b[l])
        a, u = jnp.split(h @ w_ffn_in[l], 2, axis=-1)
        h = layer_norm(ALPHA * h + (jax.nn.silu(a) * u) @ w_ffn_out[l], ln2_g[l], ln2_b[l])
    return h[:, BLOCK:]
```

```python
import functools

import numpy as np
import jax
import jax.numpy as jnp
from jax import lax
from jax.experimental import pallas as pl
from jax.experimental.pallas import tpu as pltpu

D_MODEL = 1024
N_META = 16
BLOCK = 128
HG_HEADS = 4
HG_K = 128
HG_W = HG_HEADS * HG_K
ATT_HEADS = 8
ATT_KV_HEADS = 2
HEAD_DIM = 64
ATT_QW = ATT_HEADS * HEAD_DIM
ATT_KVW = ATT_KV_HEADS * HEAD_DIM
D_FF = 2816
EPS = 1e-5
DEPTH = 1
ALPHA = (2.0 * DEPTH) ** 0.25
ROPE_THETA = 10000.0
NEG = -1e30

N_LEVELS = 7
DIAG_ID = N_LEVELS
BB = 4
FFN_TM = 512
FFN_TF = 1408
VMEM_LIMIT = 56 * 1024 * 1024


def _level_tables():
    n = BLOCK
    t = np.arange(n)[:, None]
    j = np.arange(n)[None, :]
    blocks = [(j <= t), (j > t)]
    lv = np.full((n, n), -1, np.int32)
    s = np.arange(n)[None, :]
    for l in range(N_LEVELS):
        c = n >> (l + 1)
        mid = (t // (2 * c)) * (2 * c) + c - 1
        second = (t % (2 * c)) >= c
        blocks.append(np.where(second, (j > mid) & (j <= t), (j > t) & (j <= mid)))
        same = (t // (2 * c)) == (s // (2 * c))
        lv = np.where(same & second & ((s % (2 * c)) < c), l, lv)
    lv = np.where(t == s, DIAG_ID, lv).astype(np.int32)
    w = np.concatenate(blocks, axis=0).astype(np.float32)
    return np.concatenate([w, w], axis=1), lv


def _attn_bias():
    r = np.arange(BLOCK)[:, None]
    c = np.arange(2 * BLOCK)[None, :]
    cur = (c >= BLOCK) & ((c - BLOCK) <= r)
    prev = (c < BLOCK) & (c > r)
    b0 = np.where(cur, 0.0, NEG)
    b1 = np.where(cur | prev, 0.0, NEG)
    return np.stack([b0, b1]).astype(np.float32)


def _sigmoid(x):
    return 1.0 / (1.0 + jnp.exp(-x))


def _layer_norm(x, g, b):
    mu = jnp.mean(x, axis=-1, keepdims=True)
    xc = x - mu
    var = jnp.mean(xc * xc, axis=-1, keepdims=True)
    return xc * lax.rsqrt(var + EPS) * g + b


def _dot(a, b):
    return jnp.dot(a, b, preferred_element_type=jnp.float32)


def _dot_nt(a, b):
    return lax.dot_general(a, b, (((1,), (1,)), ((), ())), preferred_element_type=jnp.float32)


def _dot_tn(a, b):
    return lax.dot_general(a, b, (((0,), (0,)), ((), ())), preferred_element_type=jnp.float32)


def _split2(x):
    hi = x.astype(jnp.bfloat16)
    lo = (x - hi.astype(jnp.float32)).astype(jnp.bfloat16)
    return jnp.concatenate([hi, lo], axis=0)


def _rope(x, cos, sin):
    w = x.shape[1]
    lane = lax.broadcasted_iota(jnp.int32, x.shape, 1)
    partner = jnp.where((lane & (HEAD_DIM - 1)) < HEAD_DIM // 2,
                        pltpu.roll(x, w - HEAD_DIM // 2, 1), pltpu.roll(x, HEAD_DIM // 2, 1))
    reps = w // cos.shape[1]
    if reps > 1:
        cos = jnp.concatenate([cos] * reps, axis=1)
        sin = jnp.concatenate([sin] * reps, axis=1)
    return x * cos + partner * sin


def _lo_hi(a, group):
    lane = lax.broadcasted_iota(jnp.int32, a.shape, 1)
    swapped = pltpu.roll(a, HEAD_DIM, 1)
    zero = jnp.zeros_like(a)
    if group == 0:
        lo = jnp.where(lane < HEAD_DIM, a, zero)
        hi = jnp.where(lane >= HEAD_DIM, swapped, zero)
    else:
        lo = jnp.where(lane < HEAD_DIM, swapped, zero)
        hi = jnp.where(lane >= HEAD_DIM, a, zero)
    return jnp.concatenate([lo, hi], axis=0)


def _meta_kernel(meta_ref, g_ref, b_ref, wf_ref, wi_ref, wk_ref, wv_ref, lb_ref, cos_ref, sin_ref,
                 ucat_ref, state_ref, kkm_ref, vvm_ref):
    h = _layer_norm(meta_ref[...], g_ref[...], b_ref[...])
    hb = h.astype(jnp.bfloat16)
    lb = lb_ref[...]
    fg = lb + (1.0 - lb) * _sigmoid(_dot(hb, wf_ref[...]))
    log_f = jnp.log(fg)
    kk = 1.0 - fg
    vb = _dot(hb, wi_ref[...]).astype(jnp.bfloat16)
    suffix = _dot(ucat_ref[...], _split2(log_f))
    ks = (kk * jnp.exp(suffix)).astype(jnp.bfloat16)
    for hh in range(HG_HEADS):
        cols = slice(hh * HG_K, (hh + 1) * HG_K)
        state_ref[hh] = _dot_tn(vb[:, cols], ks[:, cols])
    k_rot = _rope(_dot(hb, wk_ref[...]), cos_ref[...], sin_ref[...])
    v = _dot(hb, wv_ref[...])
    pad = jnp.zeros((HEAD_DIM - N_META, 2 * HEAD_DIM), jnp.float32)
    for g in range(ATT_KV_HEADS):
        for src, dst in ((k_rot, kkm_ref), (v, vvm_ref)):
            lh = _lo_hi(src, g)
            dst[g] = jnp.concatenate([lh[:N_META], pad, lh[N_META:], pad], axis=0).astype(jnp.bfloat16)


def _hgrn2_chunk(b, rows, phg_scr, state_scr, ypre_scr, wcat_ref, lv_ref, lb_ref, normg_ref):
    pq = phg_scr[rows, 0 * HG_W:1 * HG_W]
    pf = phg_scr[rows, 1 * HG_W:2 * HG_W]
    pi = phg_scr[rows, 2 * HG_W:3 * HG_W]
    pg = phg_scr[rows, 3 * HG_W:4 * HG_W]
    lb = lb_ref[...]
    q = pq * _sigmoid(pq)
    fg = lb + (1.0 - lb) * _sigmoid(pf)
    log_f = jnp.log(fg)
    kk = 1.0 - fg
    gate = pg * _sigmoid(pg)
    vb = pi.astype(jnp.bfloat16)
    e_all = _dot(wcat_ref[...], _split2(log_f))
    b_cum = e_all[0:BLOCK]
    q_in = (q * jnp.exp(b_cum)).astype(jnp.bfloat16)
    k_st = (kk * jnp.exp(e_all[BLOCK:2 * BLOCK])).astype(jnp.bfloat16)
    decay_last = jnp.exp(b_cum[BLOCK - 1:BLOCK, :])
    lv = lv_ref[...]
    normg = normg_ref[...]
    for hh in range(HG_HEADS):
        cols = slice(hh * HG_K, (hh + 1) * HG_K)
        st = state_scr[b, hh]
        o = _dot_nt(q_in[:, cols], st.astype(jnp.bfloat16))
        qh = q[:, cols]
        kh = kk[:, cols]
        a = jnp.zeros((BLOCK, BLOCK), jnp.float32)
        for l in range(N_LEVELS):
            x = jnp.exp(e_all[(2 + l) * BLOCK:(3 + l) * BLOCK, cols])
            p = _dot_nt((qh * x).astype(jnp.bfloat16), (kh * x).astype(jnp.bfloat16))
            a = jnp.where(lv == l, p, a)
        p = _dot_nt(qh.astype(jnp.bfloat16), kh.astype(jnp.bfloat16))
        a = jnp.where(lv == DIAG_ID, p, a)
        o = o + _dot(a.astype(jnp.bfloat16), vb[:, cols])
        state_scr[b, hh] = st * decay_last[:, cols] + _dot_tn(vb[:, cols], k_st[:, cols])
        r = lax.rsqrt(jnp.mean(o * o, axis=-1, keepdims=True) + EPS)
        ypre_scr[rows, cols] = (o * r * normg[:, cols] * gate[:, cols]).astype(jnp.bfloat16)


def _attention_chunk(b, rows, chunk, patt_scr, kband_scr, vband_scr, opre_scr, cos_ref, sin_ref,
                     bias_ref, sinks_ref, kkm_ref, vvm_ref):
    cos = cos_ref[...]
    sin = sin_ref[...]
    q = _rope(patt_scr[rows, 0:ATT_QW], cos, sin) * (HEAD_DIM ** -0.5)
    k = _rope(patt_scr[rows, ATT_QW:ATT_QW + ATT_KVW], cos, sin)
    v = patt_scr[rows, ATT_QW + ATT_KVW:ATT_QW + 2 * ATT_KVW]
    qb = q.astype(jnp.bfloat16)
    kband_scr[b, BLOCK:2 * BLOCK, :] = k
    vband_scr[b, BLOCK:2 * BLOCK, :] = v
    kband = kband_scr[b]
    vband = vband_scr[b]
    bias = bias_ref[jnp.minimum(chunk, 1)]
    lane = lax.broadcasted_iota(jnp.int32, (BLOCK, 2 * HEAD_DIM), 1)
    meta_mask = (lane < N_META, (lane >= HEAD_DIM) & (lane < HEAD_DIM + N_META))
    for g in range(ATT_KV_HEADS):
        qs = jnp.concatenate([qb[:, (2 * g) * 128:(2 * g + 1) * 128],
                              qb[:, (2 * g + 1) * 128:(2 * g + 2) * 128]], axis=0)
        s_band = _dot_nt(qs, _lo_hi(kband, g).astype(jnp.bfloat16))
        s_meta = _dot_nt(qs, kkm_ref[g])
        p_rows, pm_rows, rinv_rows = [], [], []
        for i in range(2):
            e_b, e_m, dinv = [], [], []
            for j in range(2):
                sink = sinks_ref[4 * g + 2 * i + j]
                sb = s_band[i * BLOCK:(i + 1) * BLOCK, j * 2 * BLOCK:(j + 1) * 2 * BLOCK] + bias
                sm = jnp.where(meta_mask[j], s_meta[i * BLOCK:(i + 1) * BLOCK, :], NEG)
                m = jnp.maximum(jnp.max(sb, axis=-1, keepdims=True), jnp.max(sm, axis=-1, keepdims=True))
                m = jnp.maximum(m, sink)
                eb = jnp.exp(sb - m)
                em = jnp.exp(sm - m)
                d = (jnp.sum(eb, axis=-1, keepdims=True) + jnp.sum(em, axis=-1, keepdims=True)
                     + jnp.exp(sink - m))
                e_b.append(eb.astype(jnp.bfloat16))
                e_m.append(em)
                dinv.append(1.0 / d)
            p_rows.append(jnp.concatenate(e_b, axis=1))
            pm_rows.append((e_m[0] + e_m[1]).astype(jnp.bfloat16))
            rinv_rows.append(jnp.where(lane < HEAD_DIM, dinv[0], dinv[1]))
        o = (_dot(jnp.concatenate(p_rows, axis=0), _lo_hi(vband, g).astype(jnp.bfloat16))
             + _dot(jnp.concatenate(pm_rows, axis=0), vvm_ref[g]))
        o = o * jnp.concatenate(rinv_rows, axis=0)
        opre_scr[rows, (2 * g) * 128:(2 * g + 1) * 128] = o[0:BLOCK].astype(jnp.bfloat16)
        opre_scr[rows, (2 * g + 1) * 128:(2 * g + 2) * 128] = o[BLOCK:2 * BLOCK].astype(jnp.bfloat16)
    kband_scr[b, 0:BLOCK, :] = kband_scr[b, BLOCK:2 * BLOCK, :]
    vband_scr[b, 0:BLOCK, :] = vband_scr[b, BLOCK:2 * BLOCK, :]


def _mixer_kernel(x_ref, cos_ref, sin_ref, bias_ref, lv_ref, wcat_ref, lng_ref, lnb_ref,
                  whg_ref, watt_ref, wgate_ref, lb_ref, normg_ref, sinks_ref,
                  wbh_ref, wba_ref, wout_ref, ln1g_ref, ln1b_ref, state0_ref, kkm_ref, vvm_ref,
                  out_ref,
                  state_scr, kband_scr, vband_scr, h_scr, phg_scr, patt_scr, ypre_scr, opre_scr):
    chunk = pl.program_id(1)
    m = BB * BLOCK

    @pl.when(chunk == 0)
    def _():
        for b in range(BB):
            state_scr[b] = state0_ref[...]
        kband_scr[...] = jnp.zeros_like(kband_scr)
        vband_scr[...] = jnp.zeros_like(vband_scr)

    h = _layer_norm(x_ref[...].reshape(m, D_MODEL), lng_ref[...], lnb_ref[...])
    h_scr[...] = h
    hb = h.astype(jnp.bfloat16)
    phg_scr[...] = _dot(hb, whg_ref[...])
    patt_scr[...] = _dot(hb, watt_ref[...])

    def per_batch(b, carry):
        rows = pl.ds(pl.multiple_of(b * BLOCK, BLOCK), BLOCK)
        _hgrn2_chunk(b, rows, phg_scr, state_scr, ypre_scr, wcat_ref, lv_ref, lb_ref, normg_ref)
        _attention_chunk(b, rows, chunk, patt_scr, kband_scr, vband_scr, opre_scr, cos_ref, sin_ref,
                         bias_ref, sinks_ref, kkm_ref, vvm_ref)
        return carry

    lax.fori_loop(0, BB, per_batch, 0)

    y_hg = _dot(ypre_scr[...], wbh_ref[...])
    y_att = _dot(opre_scr[...], wba_ref[...])
    gates = _dot(h_scr[...].astype(jnp.bfloat16), wgate_ref[...])
    mix_in = _sigmoid(gates[:, :D_MODEL]) * y_hg + _sigmoid(gates[:, D_MODEL:]) * y_att
    mix = _dot(mix_in.astype(jnp.bfloat16), wout_ref[...])
    out = _layer_norm(ALPHA * h_scr[...] + mix, ln1g_ref[...], ln1b_ref[...])
    out_ref[...] = out.reshape(BB, BLOCK, D_MODEL)


def _ffn_kernel(h_ref, wa_ref, wu_ref, wo_ref, g_ref, b_ref, out_ref):
    h = h_ref[...]
    hb = h.astype(jnp.bfloat16)
    acc = jnp.zeros((FFN_TM, D_MODEL), jnp.float32)
    for c in range(D_FF // FFN_TF):
        sl = slice(c * FFN_TF, (c + 1) * FFN_TF)
        a = _dot(hb, wa_ref[:, sl])
        u = _dot(hb, wu_ref[:, sl])
        hid = (a * _sigmoid(a) * u).astype(jnp.bfloat16)
        acc = acc + _dot(hid, wo_ref[sl, :])
    out_ref[...] = _layer_norm(ALPHA * h + acc, g_ref[...], b_ref[...])


def _const_spec(shape):
    nd = len(shape)
    return pl.BlockSpec(shape, lambda *_: (0,) * nd, pipeline_mode=pl.Buffered(1))


def kernel(x, meta_tokens, ln_emb_g, ln_emb_b, w_in, hg_lower_bounds, hg_norm_g, attn_sinks,
           w_branch_hg, w_branch_attn, w_out, ln1_g, ln1_b, w_ffn_in, w_ffn_out, ln2_g, ln2_b):
    B, S, D = x.shape
    assert D == D_MODEL and S % BLOCK == 0 and B % BB == 0 and (B * S) % FFN_TM == 0
    n_chunks = S // BLOCK
    f32, bf16 = jnp.float32, jnp.bfloat16
    row = lambda v: v.astype(f32).reshape(1, -1)

    lb = jnp.cumsum(jax.nn.softmax(hg_lower_bounds.astype(f32), axis=0), axis=0)[0].reshape(1, HG_W)
    normg = jnp.tile(hg_norm_g[0].astype(f32), HG_HEADS).reshape(1, HG_W)
    w0 = w_in[0].astype(bf16)
    o_att = 4 * HG_W
    o_gate = o_att + ATT_QW + 2 * ATT_KVW
    w_hg, w_att, w_gate = w0[:, :o_att], w0[:, o_att:o_gate], w0[:, o_gate:]
    half = HEAD_DIM // 2
    inv = ROPE_THETA ** (-jnp.arange(half, dtype=f32) / half)
    ang = jnp.arange(S + N_META, dtype=jnp.int32).astype(f32)[:, None] * inv[None, :]
    cos_t = jnp.tile(jnp.cos(ang), (1, 4))
    sin_t = jnp.tile(jnp.concatenate([-jnp.sin(ang), jnp.sin(ang)], axis=1), (1, 2))
    wcat_np, lv_np = _level_tables()
    wcat = jnp.asarray(wcat_np, bf16)
    lv = jnp.asarray(lv_np)
    ucat = jnp.asarray(np.tile(np.triu(np.ones((N_META, N_META), np.float32), 1), (1, 2)), bf16)
    bias = jnp.asarray(_attn_bias())
    cparams = functools.partial(pltpu.CompilerParams, vmem_limit_bytes=VMEM_LIMIT)

    state0, kkm, vvm = pl.pallas_call(
        _meta_kernel,
        out_shape=(jax.ShapeDtypeStruct((HG_HEADS, HG_K, HG_K), f32),
                   jax.ShapeDtypeStruct((ATT_KV_HEADS, BLOCK, 2 * HEAD_DIM), bf16),
                   jax.ShapeDtypeStruct((ATT_KV_HEADS, BLOCK, 2 * HEAD_DIM), bf16)),
        name="meta",
    )(meta_tokens.astype(f32), row(ln_emb_g), row(ln_emb_b),
      w_hg[:, HG_W:2 * HG_W], w_hg[:, 2 * HG_W:3 * HG_W],
      w_att[:, ATT_QW:ATT_QW + ATT_KVW], w_att[:, ATT_QW + ATT_KVW:],
      lb, cos_t[:N_META], sin_t[:N_META], ucat)

    m = BB * BLOCK
    h1 = pl.pallas_call(
        _mixer_kernel,
        grid=(B // BB, n_chunks),
        in_specs=[
            pl.BlockSpec((BB, BLOCK, D), lambda i, c: (i, c, 0)),
            pl.BlockSpec((BLOCK, 2 * HEAD_DIM), lambda i, c: (c, 0)),
            pl.BlockSpec((BLOCK, 2 * HEAD_DIM), lambda i, c: (c, 0)),
            _const_spec(bias.shape), _const_spec(lv.shape), _const_spec(wcat.shape),
            _const_spec((1, D)), _const_spec((1, D)),
            _const_spec(w_hg.shape), _const_spec(w_att.shape), _const_spec(w_gate.shape),
            _const_spec((1, HG_W)), _const_spec((1, HG_W)),
            pl.BlockSpec(memory_space=pltpu.SMEM),
            _const_spec((HG_W, D)), _const_spec((ATT_QW, D)), _const_spec((D, D)),
            _const_spec((1, D)), _const_spec((1, D)),
            _const_spec(state0.shape), _const_spec(kkm.shape), _const_spec(vvm.shape),
        ],
        out_specs=pl.BlockSpec((BB, BLOCK, D), lambda i, c: (i, c, 0)),
        out_shape=jax.ShapeDtypeStruct((B, S, D), f32),
        scratch_shapes=[
            pltpu.VMEM((BB, HG_HEADS, HG_K, HG_K), f32),
            pltpu.VMEM((BB, 2 * BLOCK, ATT_KVW), f32),
            pltpu.VMEM((BB, 2 * BLOCK, ATT_KVW), f32),
            pltpu.VMEM((m, D), f32),
            pltpu.VMEM((m, 4 * HG_W), f32),
            pltpu.VMEM((m, ATT_QW + 2 * ATT_KVW), f32),
            pltpu.VMEM((m, HG_W), bf16),
            pltpu.VMEM((m, ATT_QW), bf16),
        ],
        compiler_params=cparams(dimension_semantics=("arbitrary", "arbitrary")),
        name="mixer",
    )(x.astype(f32), cos_t[N_META:], sin_t[N_META:], bias, lv, wcat, row(ln_emb_g), row(ln_emb_b),
      w_hg, w_att, w_gate, lb, normg, attn_sinks[0].astype(f32),
      w_branch_hg[0].astype(bf16), w_branch_attn[0].astype(bf16), w_out[0].astype(bf16),
      row(ln1_g[0]), row(ln1_b[0]), state0, kkm, vvm)

    wf = w_ffn_in[0].astype(bf16)
    out = pl.pallas_call(
        _ffn_kernel,
        grid=(B * S // FFN_TM,),
        in_specs=[
            pl.BlockSpec((FFN_TM, D), lambda i: (i, 0)),
            _const_spec((D, D_FF)), _const_spec((D, D_FF)), _const_spec((D_FF, D)),
            _const_spec((1, D)), _const_spec((1, D)),
        ],
        out_specs=pl.BlockSpec((FFN_TM, D), lambda i: (i, 0)),
        out_shape=jax.ShapeDtypeStruct((B * S, D), f32),
        compiler_params=cparams(dimension_semantics=("arbitrary",)),
        name="ffn",
    )(h1.reshape(B * S, D), wf[:, :D_FF], wf[:, D_FF:], w_ffn_out[0].astype(bf16),
      row(ln2_g[0]), row(ln2_b[0]))
    return out.reshape(B, S, D)
```

```python
import functools

import numpy as np
import jax
import jax.numpy as jnp
from jax import lax
from jax.experimental import pallas as pl
from jax.experimental.pallas import tpu as pltpu

D_MODEL = 1024
N_META = 16
BLOCK = 128
HG_HEADS = 4
HG_K = 128
HG_W = HG_HEADS * HG_K
ATT_HEADS = 8
ATT_KV_HEADS = 2
HEAD_DIM = 64
ATT_QW = ATT_HEADS * HEAD_DIM
ATT_KVW = ATT_KV_HEADS * HEAD_DIM
D_FF = 2816
EPS = 1e-5
DEPTH = 1
ALPHA = (2.0 * DEPTH) ** 0.25
ROPE_THETA = 10000.0
NEG = -1e30
LOG2E = 1.4426950408889634

N_LEVELS = 7
DIAG_ID = N_LEVELS
BB = 4
IB = 2
FFN_TM = 512
MXU_N = 256
FFN_SPLITS = (0, 6 * MXU_N, D_FF)
DENSE_TN = 512
VMEM_LIMIT = 56 * 1024 * 1024


def _level_tables():
    n = BLOCK
    t = np.arange(n)[:, None]
    j = np.arange(n)[None, :]
    blocks = [(j <= t), (j > t)]
    lv = np.full((n, n), -1, np.int32)
    s = np.arange(n)[None, :]
    for l in range(N_LEVELS):
        c = n >> (l + 1)
        mid = (t // (2 * c)) * (2 * c) + c - 1
        second = (t % (2 * c)) >= c
        blocks.append(np.where(second, (j > mid) & (j <= t), (j > t) & (j <= mid)))
        same = (t // (2 * c)) == (s // (2 * c))
        lv = np.where(same & second & ((s % (2 * c)) < c), l, lv)
    lv = np.where(t == s, DIAG_ID, lv).astype(np.int32)
    w = np.concatenate(blocks, axis=0).astype(np.float32)
    return np.concatenate([w, w], axis=1), lv


def _attn_bias():
    r = np.arange(BLOCK)[:, None]
    c = np.arange(2 * BLOCK)[None, :]
    cur = (c >= BLOCK) & ((c - BLOCK) <= r)
    prev = (c < BLOCK) & (c > r)
    b0 = np.where(cur, 0.0, NEG)
    b1 = np.where(cur | prev, 0.0, NEG)
    return np.stack([b0, b1]).astype(np.float32)


def _sigmoid(x):
    return 1.0 / (1.0 + jnp.exp(-x))


def _layer_norm(x, g, b):
    mu = jnp.mean(x, axis=-1, keepdims=True)
    xc = x - mu
    var = jnp.mean(xc * xc, axis=-1, keepdims=True)
    return xc * lax.rsqrt(var + EPS) * g + b


def _dot(a, b):
    return jnp.dot(a, b, preferred_element_type=jnp.float32)


def _dot_nt(a, b):
    return lax.dot_general(a, b, (((1,), (1,)), ((), ())), preferred_element_type=jnp.float32)


def _dot_tn(a, b):
    return lax.dot_general(a, b, (((0,), (0,)), ((), ())), preferred_element_type=jnp.float32)


def _split2(x):
    hi = x.astype(jnp.bfloat16)
    lo = (x - hi.astype(jnp.float32)).astype(jnp.bfloat16)
    return jnp.concatenate([hi, lo], axis=0)


def _rope(x, cos, sin):
    w = x.shape[1]
    lane = lax.broadcasted_iota(jnp.int32, x.shape, 1)
    partner = jnp.where((lane & (HEAD_DIM - 1)) < HEAD_DIM // 2,
                        pltpu.roll(x, w - HEAD_DIM // 2, 1), pltpu.roll(x, HEAD_DIM // 2, 1))
    reps = w // cos.shape[1]
    if reps > 1:
        cos = jnp.concatenate([cos] * reps, axis=1)
        sin = jnp.concatenate([sin] * reps, axis=1)
    return x * cos + partner * sin


def _lo_hi(a, group):
    lane = lax.broadcasted_iota(jnp.int32, a.shape, 1)
    swapped = pltpu.roll(a, HEAD_DIM, 1)
    zero = jnp.zeros_like(a)
    if group == 0:
        lo = jnp.where(lane < HEAD_DIM, a, zero)
        hi = jnp.where(lane >= HEAD_DIM, swapped, zero)
    else:
        lo = jnp.where(lane < HEAD_DIM, swapped, zero)
        hi = jnp.where(lane >= HEAD_DIM, a, zero)
    return jnp.concatenate([lo, hi], axis=0)


def _meta_kernel(meta_ref, g_ref, b_ref, wf_ref, wi_ref, wk_ref, wv_ref, lb_ref, cos_ref, sin_ref,
                 ucat_ref, state_ref, kkm_ref, vvm_ref):
    h = _layer_norm(meta_ref[...], g_ref[...], b_ref[...])
    hb = h.astype(jnp.bfloat16)
    lb = lb_ref[...]
    fg = lb + (1.0 - lb) * _sigmoid(_dot(hb, wf_ref[...]))
    log2_f = jnp.log2(fg)
    kk = 1.0 - fg
    vb = _dot(hb, wi_ref[...]).astype(jnp.bfloat16)
    suffix = _dot(ucat_ref[...], _split2(log2_f))
    ks = (kk * jnp.exp2(suffix)).astype(jnp.bfloat16)
    for hh in range(HG_HEADS):
        cols = slice(hh * HG_K, (hh + 1) * HG_K)
        state_ref[hh] = _dot_tn(vb[:, cols], ks[:, cols])
    k_rot = _rope(_dot(hb, wk_ref[...]), cos_ref[...], sin_ref[...])
    v = _dot(hb, wv_ref[...])
    pad = jnp.zeros((HEAD_DIM - N_META, 2 * HEAD_DIM), jnp.float32)
    for g in range(ATT_KV_HEADS):
        for src, dst in ((k_rot, kkm_ref), (v, vvm_ref)):
            lh = _lo_hi(src, g)
            dst[g] = jnp.concatenate([lh[:N_META], pad, lh[N_META:], pad], axis=0).astype(jnp.bfloat16)


def _hgrn2_pre(rows, phg_scr, wcat_ref, lb_ref):
    pq = phg_scr[rows, 0 * HG_W:1 * HG_W]
    pf = phg_scr[rows, 1 * HG_W:2 * HG_W]
    pi = phg_scr[rows, 2 * HG_W:3 * HG_W]
    pg = phg_scr[rows, 3 * HG_W:4 * HG_W]
    lb = lb_ref[...]
    q = pq * _sigmoid(pq)
    fg = lb + (1.0 - lb) * _sigmoid(pf)
    log2_f = jnp.log2(fg)
    kk = 1.0 - fg
    e_all = _dot(wcat_ref[...], _split2(log2_f))
    b_cum = e_all[0:BLOCK]
    return dict(
        q=q, kk=kk, gate=pg * _sigmoid(pg), vb=pi.astype(jnp.bfloat16), e_all=e_all,
        q_in=(q * jnp.exp2(b_cum)).astype(jnp.bfloat16),
        k_st=(kk * jnp.exp2(e_all[BLOCK:2 * BLOCK])).astype(jnp.bfloat16),
        decay_last=jnp.exp2(b_cum[BLOCK - 1:BLOCK, :]))


def _hgrn2_head(pre, hh, b, rows, state_scr, ypre_scr, lv_ref, normg_ref):
    cols = slice(hh * HG_K, (hh + 1) * HG_K)
    lv = lv_ref[...]
    st = state_scr[b, hh]
    o = _dot_nt(pre["q_in"][:, cols], st.astype(jnp.bfloat16))
    qh = pre["q"][:, cols]
    kh = pre["kk"][:, cols]
    vb = pre["vb"][:, cols]
    a = jnp.zeros((BLOCK, BLOCK), jnp.float32)
    for l in range(N_LEVELS):
        x = jnp.exp2(pre["e_all"][(2 + l) * BLOCK:(3 + l) * BLOCK, cols])
        p = _dot_nt((qh * x).astype(jnp.bfloat16), (kh * x).astype(jnp.bfloat16))
        a = jnp.where(lv == l, p, a)
    p = _dot_nt(qh.astype(jnp.bfloat16), kh.astype(jnp.bfloat16))
    a = jnp.where(lv == DIAG_ID, p, a)
    o = o + _dot(a.astype(jnp.bfloat16), vb)
    state_scr[b, hh] = st * pre["decay_last"][:, cols] + _dot_tn(vb, pre["k_st"][:, cols])
    r = lax.rsqrt(jnp.mean(o * o, axis=-1, keepdims=True) + EPS)
    ypre_scr[rows, cols] = (o * r * normg_ref[:, cols] * pre["gate"][:, cols]).astype(jnp.bfloat16)


def _attn_pre(b, rows, chunk, patt_scr, kband_scr, vband_scr, cos_ref, sin_ref, bias_ref):
    cos = cos_ref[...]
    sin = sin_ref[...]
    q = _rope(patt_scr[rows, 0:ATT_QW], cos, sin) * (HEAD_DIM ** -0.5 * LOG2E)
    k = _rope(patt_scr[rows, ATT_QW:ATT_QW + ATT_KVW], cos, sin)
    v = patt_scr[rows, ATT_QW + ATT_KVW:ATT_QW + 2 * ATT_KVW]
    kband = jnp.concatenate([kband_scr[b], k], axis=0)
    vband = jnp.concatenate([vband_scr[b], v], axis=0)
    kband_scr[b] = k
    vband_scr[b] = v
    return dict(qb=q.astype(jnp.bfloat16), kband=kband, vband=vband,
                bias=bias_ref[jnp.minimum(chunk, 1)])


def _attn_group(pre, g, rows, opre_scr, sinks_ref, kkm_ref, vvm_ref):
    qb = pre["qb"]
    bias = pre["bias"]
    lane = lax.broadcasted_iota(jnp.int32, (BLOCK, 2 * HEAD_DIM), 1)
    meta_mask = (lane < N_META, (lane >= HEAD_DIM) & (lane < HEAD_DIM + N_META))
    qs = jnp.concatenate([qb[:, (2 * g) * 128:(2 * g + 1) * 128],
                          qb[:, (2 * g + 1) * 128:(2 * g + 2) * 128]], axis=0)
    s_band = _dot_nt(qs, _lo_hi(pre["kband"], g).astype(jnp.bfloat16))
    s_meta = _dot_nt(qs, kkm_ref[g])
    p_rows, pm_rows, rinv_rows = [], [], []
    for i in range(2):
        e_b, e_m, dinv = [], [], []
        for j in range(2):
            sink = sinks_ref[4 * g + 2 * i + j] * LOG2E
            sb = s_band[i * BLOCK:(i + 1) * BLOCK, j * 2 * BLOCK:(j + 1) * 2 * BLOCK] + bias
            sm = jnp.where(meta_mask[j], s_meta[i * BLOCK:(i + 1) * BLOCK, :], NEG)
            m = jnp.max(jnp.maximum(jnp.maximum(sb[:, :BLOCK], sb[:, BLOCK:]), sm), axis=-1, keepdims=True)
            m = jnp.maximum(m, sink)
            eb = jnp.exp2(sb - m)
            em = jnp.exp2(sm - m)
            d = (jnp.sum(eb[:, :BLOCK] + eb[:, BLOCK:] + em, axis=-1, keepdims=True)
                 + jnp.exp2(sink - m))
            e_b.append(eb.astype(jnp.bfloat16))
            e_m.append(em)
            dinv.append(1.0 / d)
        p_rows.append(jnp.concatenate(e_b, axis=1))
        pm_rows.append((e_m[0] + e_m[1]).astype(jnp.bfloat16))
        rinv_rows.append(jnp.where(lane < HEAD_DIM, dinv[0], dinv[1]))
    o = (_dot(jnp.concatenate(p_rows, axis=0), _lo_hi(pre["vband"], g).astype(jnp.bfloat16))
         + _dot(jnp.concatenate(pm_rows, axis=0), vvm_ref[g]))
    o = o * jnp.concatenate(rinv_rows, axis=0)
    opre_scr[rows, (2 * g) * 128:(2 * g + 1) * 128] = o[0:BLOCK].astype(jnp.bfloat16)
    opre_scr[rows, (2 * g + 1) * 128:(2 * g + 2) * 128] = o[BLOCK:2 * BLOCK].astype(jnp.bfloat16)


def _mixer_kernel(x_ref, cos_ref, sin_ref, bias_ref, lv_ref, wcat_ref, lng_ref, lnb_ref,
                  whg_ref, watt_ref, wgate_ref, lb_ref, normg_ref, sinks_ref,
                  wbh_ref, wba_ref, wout_ref, ln1g_ref, ln1b_ref, state0_ref, kkm_ref, vvm_ref,
                  out_ref,
                  state_scr, kband_scr, vband_scr, h_scr, hb_scr, phg_scr, patt_scr, ypre_scr, opre_scr,
                  mixin_scr, res_scr):
    chunk = pl.program_id(1)
    m = BB * BLOCK

    @pl.when(chunk == 0)
    def _():
        for b in range(BB):
            state_scr[b] = state0_ref[...]
        kband_scr[...] = jnp.zeros_like(kband_scr)
        vband_scr[...] = jnp.zeros_like(vband_scr)

    h = _layer_norm(x_ref[...].reshape(m, D_MODEL), lng_ref[...], lnb_ref[...])
    h_scr[...] = h
    hb_scr[...] = h.astype(jnp.bfloat16)
    for k in range(4 * HG_W // DENSE_TN):
        cols = slice(k * DENSE_TN, (k + 1) * DENSE_TN)
        phg_scr[:, cols] = _dot(hb_scr[...], whg_ref[:, cols])
    patt_scr[...] = _dot(hb_scr[...], watt_ref[...])

    def per_group(i, carry):
        pieces = []
        for bb in range(IB):
            b = i * IB + bb
            rows = pl.ds(pl.multiple_of(b * BLOCK, BLOCK), BLOCK)
            hg = _hgrn2_pre(rows, phg_scr, wcat_ref, lb_ref)
            at = _attn_pre(b, rows, chunk, patt_scr, kband_scr, vband_scr, cos_ref, sin_ref, bias_ref)
            pieces.append(
                [functools.partial(_hgrn2_head, hg, hh, b, rows, state_scr, ypre_scr, lv_ref, normg_ref)
                 for hh in range(HG_HEADS)]
                + [functools.partial(_attn_group, at, g, rows, opre_scr, sinks_ref, kkm_ref, vvm_ref)
                   for g in range(ATT_KV_HEADS)])
        for group in zip(*pieces):
            for piece in group:
                piece()
        return carry

    lax.fori_loop(0, BB // IB, per_group, 0)

    for k in range(D_MODEL // DENSE_TN):
        cols = slice(k * DENSE_TN, (k + 1) * DENSE_TN)
        cols2 = slice(D_MODEL + k * DENSE_TN, D_MODEL + (k + 1) * DENSE_TN)
        g_hg = _sigmoid(_dot(hb_scr[...], wgate_ref[:, cols]))
        g_att = _sigmoid(_dot(hb_scr[...], wgate_ref[:, cols2]))
        y_hg = _dot(ypre_scr[...], wbh_ref[:, cols])
        y_att = _dot(opre_scr[...], wba_ref[:, cols])
        mixin_scr[:, cols] = (g_hg * y_hg + g_att * y_att).astype(jnp.bfloat16)
    for k in range(D_MODEL // DENSE_TN):
        cols = slice(k * DENSE_TN, (k + 1) * DENSE_TN)
        res_scr[:, cols] = ALPHA * h_scr[:, cols] + _dot(mixin_scr[...], wout_ref[:, cols])
    out = _layer_norm(res_scr[...], ln1g_ref[...], ln1b_ref[...])
    out_ref[...] = out.reshape(BB, BLOCK, D_MODEL)


def _ffn_kernel(h_ref, wa_ref, wu_ref, wo_ref, g_ref, b_ref, out_ref):
    h = h_ref[...]
    hb = h.astype(jnp.bfloat16)
    acc = jnp.zeros((FFN_TM, D_MODEL), jnp.float32)
    for lo, hi in zip(FFN_SPLITS[:-1], FFN_SPLITS[1:]):
        sl = slice(lo, hi)
        a = _dot(hb, wa_ref[:, sl])
        u = _dot(hb, wu_ref[:, sl])
        hid = (a * _sigmoid(a) * u).astype(jnp.bfloat16)
        acc = acc + _dot(hid, wo_ref[sl, :])
    out_ref[...] = _layer_norm(ALPHA * h + acc, g_ref[...], b_ref[...])


def _const_spec(shape):
    nd = len(shape)
    return pl.BlockSpec(shape, lambda *_: (0,) * nd, pipeline_mode=pl.Buffered(1))


def kernel(x, meta_tokens, ln_emb_g, ln_emb_b, w_in, hg_lower_bounds, hg_norm_g, attn_sinks,
           w_branch_hg, w_branch_attn, w_out, ln1_g, ln1_b, w_ffn_in, w_ffn_out, ln2_g, ln2_b):
    B, S, D = x.shape
    assert D == D_MODEL and S % BLOCK == 0 and B % IB == 0 and (B * S) % FFN_TM == 0
    n_chunks = S // BLOCK
    f32, bf16 = jnp.float32, jnp.bfloat16
    row = lambda v: v.astype(f32).reshape(1, -1)

    lb = jnp.cumsum(jax.nn.softmax(hg_lower_bounds.astype(f32), axis=0), axis=0)[0].reshape(1, HG_W)
    normg = jnp.tile(hg_norm_g[0].astype(f32), HG_HEADS).reshape(1, HG_W)
    w0 = w_in[0].astype(bf16)
    o_att = 4 * HG_W
    o_gate = o_att + ATT_QW + 2 * ATT_KVW
    w_hg, w_att, w_gate = w0[:, :o_att], w0[:, o_att:o_gate], w0[:, o_gate:]
    half = HEAD_DIM // 2
    inv = ROPE_THETA ** (-jnp.arange(half, dtype=f32) / half)
    ang = jnp.arange(S + N_META, dtype=jnp.int32).astype(f32)[:, None] * inv[None, :]
    cos_t = jnp.tile(jnp.cos(ang), (1, 4))
    sin_t = jnp.tile(jnp.concatenate([-jnp.sin(ang), jnp.sin(ang)], axis=1), (1, 2))
    wcat_np, lv_np = _level_tables()
    wcat = jnp.asarray(wcat_np, bf16)
    lv = jnp.asarray(lv_np)
    ucat = jnp.asarray(np.tile(np.triu(np.ones((N_META, N_META), np.float32), 1), (1, 2)), bf16)
    bias = jnp.asarray(_attn_bias())
    cparams = functools.partial(pltpu.CompilerParams, vmem_limit_bytes=VMEM_LIMIT)

    state0, kkm, vvm = pl.pallas_call(
        _meta_kernel,
        out_shape=(jax.ShapeDtypeStruct((HG_HEADS, HG_K, HG_K), f32),
                   jax.ShapeDtypeStruct((ATT_KV_HEADS, BLOCK, 2 * HEAD_DIM), bf16),
                   jax.ShapeDtypeStruct((ATT_KV_HEADS, BLOCK, 2 * HEAD_DIM), bf16)),
        name="meta",
    )(meta_tokens.astype(f32), row(ln_emb_g), row(ln_emb_b),
      w_hg[:, HG_W:2 * HG_W], w_hg[:, 2 * HG_W:3 * HG_W],
      w_att[:, ATT_QW:ATT_QW + ATT_KVW], w_att[:, ATT_QW + ATT_KVW:],
      lb, cos_t[:N_META], sin_t[:N_META], ucat)

    m = BB * BLOCK
    h1 = pl.pallas_call(
        _mixer_kernel,
        grid=(B // BB, n_chunks),
        in_specs=[
            pl.BlockSpec((BB, BLOCK, D), lambda i, c: (i, c, 0)),
            pl.BlockSpec((BLOCK, 2 * HEAD_DIM), lambda i, c: (c, 0)),
            pl.BlockSpec((BLOCK, 2 * HEAD_DIM), lambda i, c: (c, 0)),
            _const_spec(bias.shape), _const_spec(lv.shape), _const_spec(wcat.shape),
            _const_spec((1, D)), _const_spec((1, D)),
            _const_spec(w_hg.shape), _const_spec(w_att.shape), _const_spec(w_gate.shape),
            _const_spec((1, HG_W)), _const_spec((1, HG_W)),
            pl.BlockSpec(memory_space=pltpu.SMEM),
            _const_spec((HG_W, D)), _const_spec((ATT_QW, D)), _const_spec((D, D)),
            _const_spec((1, D)), _const_spec((1, D)),
            _const_spec(state0.shape), _const_spec(kkm.shape), _const_spec(vvm.shape),
        ],
        out_specs=pl.BlockSpec((BB, BLOCK, D), lambda i, c: (i, c, 0)),
        out_shape=jax.ShapeDtypeStruct((B, S, D), f32),
        scratch_shapes=[
            pltpu.VMEM((BB, HG_HEADS, HG_K, HG_K), f32),
            pltpu.VMEM((BB, BLOCK, ATT_KVW), f32),
            pltpu.VMEM((BB, BLOCK, ATT_KVW), f32),
            pltpu.VMEM((m, D), f32),
            pltpu.VMEM((m, D), bf16),
            pltpu.VMEM((m, 4 * HG_W), f32),
            pltpu.VMEM((m, ATT_QW + 2 * ATT_KVW), f32),
            pltpu.VMEM((m, HG_W), bf16),
            pltpu.VMEM((m, ATT_QW), bf16),
            pltpu.VMEM((m, D), bf16),
            pltpu.VMEM((m, D), f32),
        ],
        compiler_params=cparams(dimension_semantics=("arbitrary", "arbitrary")),
        name="mixer",
    )(x.astype(f32), cos_t[N_META:], sin_t[N_META:], bias, lv, wcat, row(ln_emb_g), row(ln_emb_b),
      w_hg, w_att, w_gate, lb, normg, attn_sinks[0].astype(f32),
      w_branch_hg[0].astype(bf16), w_branch_attn[0].astype(bf16), w_out[0].astype(bf16),
      row(ln1_g[0]), row(ln1_b[0]), state0, kkm, vvm)

    wf = w_ffn_in[0].astype(bf16)
    out = pl.pallas_call(
        _ffn_kernel,
        grid=(B * S // FFN_TM,),
        in_specs=[
            pl.BlockSpec((FFN_TM, D), lambda i: (i, 0)),
            _const_spec((D, D_FF)), _const_spec((D, D_FF)), _const_spec((D_FF, D)),
            _const_spec((1, D)), _const_spec((1, D)),
        ],
        out_specs=pl.BlockSpec((FFN_TM, D), lambda i: (i, 0)),
        out_shape=jax.ShapeDtypeStruct((B * S, D), f32),
        compiler_params=cparams(dimension_semantics=("arbitrary",)),
        name="ffn",
    )(h1.reshape(B * S, D), wf[:, :D_FF], wf[:, D_FF:], w_ffn_out[0].astype(bf16),
      row(ln2_g[0]), row(ln2_b[0]))
    return out.reshape(B, S, D)
```

```python
import functools

import numpy as np
import jax
import jax.numpy as jnp
from jax import lax
from jax.experimental import pallas as pl
from jax.experimental.pallas import tpu as pltpu

D_MODEL = 1024
N_META = 16
BLOCK = 128
HG_HEADS = 4
HG_K = 128
HG_W = HG_HEADS * HG_K
ATT_HEADS = 8
ATT_KV_HEADS = 2
HEAD_DIM = 64
ATT_QW = ATT_HEADS * HEAD_DIM
ATT_KVW = ATT_KV_HEADS * HEAD_DIM
D_FF = 2816
EPS = 1e-5
DEPTH = 1
ALPHA = (2.0 * DEPTH) ** 0.25
ROPE_THETA = 10000.0
NEG = -1e30
LOG2E = 1.4426950408889634

N_LEVELS = 7
DIAG_ID = N_LEVELS
N_MXU_LEVELS = 2
BB = 4
IB = 2
FFN_TM = 512
MXU_N = 256
FFN_SPLITS = (0, 6 * MXU_N, D_FF)
DENSE_TN = 512
VMEM_LIMIT = 56 * 1024 * 1024


def _level_tables():
    n = BLOCK
    t = np.arange(n)[:, None]
    j = np.arange(n)[None, :]
    blocks = [(j <= t)]
    lv = np.full((n, n), -1, np.int32)
    s = np.arange(n)[None, :]
    for l in range(N_LEVELS):
        c = n >> (l + 1)
        mid = (t // (2 * c)) * (2 * c) + c - 1
        second = (t % (2 * c)) >= c
        if l >= N_LEVELS - N_MXU_LEVELS:
            blocks.append(np.where(second, (j > mid) & (j <= t), (j > t) & (j <= mid)))
        same = (t // (2 * c)) == (s // (2 * c))
        lv = np.where(same & second & ((s % (2 * c)) < c), l, lv)
    lv = np.where(t == s, DIAG_ID, lv).astype(np.int32)
    w = np.concatenate(blocks, axis=0).astype(np.float32)
    return np.concatenate([w, w], axis=1), lv


def _attn_bias():
    r = np.arange(BLOCK)[:, None]
    c = np.arange(2 * BLOCK)[None, :]
    cur = (c >= BLOCK) & ((c - BLOCK) <= r)
    prev = (c < BLOCK) & (c > r)
    b0 = np.where(cur, 0.0, NEG)
    b1 = np.where(cur | prev, 0.0, NEG)
    return np.stack([b0, b1]).astype(np.float32)


def _sigmoid(x):
    return 1.0 / (1.0 + jnp.exp(-x))


def _layer_norm(x, g, b):
    mu = jnp.mean(x, axis=-1, keepdims=True)
    xc = x - mu
    var = jnp.mean(xc * xc, axis=-1, keepdims=True)
    return xc * lax.rsqrt(var + EPS) * g + b


def _dot(a, b):
    return jnp.dot(a, b, preferred_element_type=jnp.float32)


def _dot_nt(a, b):
    return lax.dot_general(a, b, (((1,), (1,)), ((), ())), preferred_element_type=jnp.float32)


def _dot_tn(a, b):
    return lax.dot_general(a, b, (((0,), (0,)), ((), ())), preferred_element_type=jnp.float32)


def _split2(x):
    hi = x.astype(jnp.bfloat16)
    lo = (x - hi.astype(jnp.float32)).astype(jnp.bfloat16)
    return jnp.concatenate([hi, lo], axis=0)


def _rope(x, cos, sin):
    w = x.shape[1]
    lane = lax.broadcasted_iota(jnp.int32, x.shape, 1)
    partner = jnp.where((lane & (HEAD_DIM - 1)) < HEAD_DIM // 2,
                        pltpu.roll(x, w - HEAD_DIM // 2, 1), pltpu.roll(x, HEAD_DIM // 2, 1))
    reps = w // cos.shape[1]
    if reps > 1:
        cos = jnp.concatenate([cos] * reps, axis=1)
        sin = jnp.concatenate([sin] * reps, axis=1)
    return x * cos + partner * sin


def _lo_hi(a, group):
    lane = lax.broadcasted_iota(jnp.int32, a.shape, 1)
    swapped = pltpu.roll(a, HEAD_DIM, 1)
    zero = jnp.zeros_like(a)
    if group == 0:
        lo = jnp.where(lane < HEAD_DIM, a, zero)
        hi = jnp.where(lane >= HEAD_DIM, swapped, zero)
    else:
        lo = jnp.where(lane < HEAD_DIM, swapped, zero)
        hi = jnp.where(lane >= HEAD_DIM, a, zero)
    return jnp.concatenate([lo, hi], axis=0)


def _meta_kernel(meta_ref, g_ref, b_ref, wf_ref, wi_ref, wk_ref, wv_ref, lb_ref, cos_ref, sin_ref,
                 ucat_ref, state_ref, kkm_ref, vvm_ref):
    h = _layer_norm(meta_ref[...], g_ref[...], b_ref[...])
    hb = h.astype(jnp.bfloat16)
    lb = lb_ref[...]
    fg = lb + (1.0 - lb) * _sigmoid(_dot(hb, wf_ref[...]))
    log2_f = jnp.log2(fg)
    kk = 1.0 - fg
    vb = _dot(hb, wi_ref[...]).astype(jnp.bfloat16)
    suffix = _dot(ucat_ref[...], _split2(log2_f))
    ks = (kk * jnp.exp2(suffix)).astype(jnp.bfloat16)
    for hh in range(HG_HEADS):
        cols = slice(hh * HG_K, (hh + 1) * HG_K)
        state_ref[hh] = _dot_tn(vb[:, cols], ks[:, cols])
    k_rot = _rope(_dot(hb, wk_ref[...]), cos_ref[...], sin_ref[...])
    v = _dot(hb, wv_ref[...])
    pad = jnp.zeros((HEAD_DIM - N_META, 2 * HEAD_DIM), jnp.float32)
    for g in range(ATT_KV_HEADS):
        for src, dst in ((k_rot, kkm_ref), (v, vvm_ref)):
            lh = _lo_hi(src, g)
            dst[g] = jnp.concatenate([lh[:N_META], pad, lh[N_META:], pad], axis=0).astype(jnp.bfloat16)


def _hgrn2_pre(rows, phg_scr, wcat_ref, lb_ref):
    pq = phg_scr[rows, 0 * HG_W:1 * HG_W]
    pf = phg_scr[rows, 1 * HG_W:2 * HG_W]
    pi = phg_scr[rows, 2 * HG_W:3 * HG_W]
    pg = phg_scr[rows, 3 * HG_W:4 * HG_W]
    lb = lb_ref[...]
    q = pq * _sigmoid(pq)
    fg = lb + (1.0 - lb) * _sigmoid(pf)
    log2_f = jnp.log2(fg)
    kk = 1.0 - fg
    e_mxu = _dot(wcat_ref[...], _split2(log2_f))
    b_cum = e_mxu[0:BLOCK]
    b_last = b_cum[BLOCK - 1:BLOCK, :]
    e_lv = []
    for l in range(N_LEVELS):
        c = BLOCK >> (l + 1)
        if l < N_LEVELS - N_MXU_LEVELS:
            b3 = b_cum.reshape(BLOCK // (2 * c), 2 * c, HG_W)
            e_lv.append((-jnp.abs(b3 - b3[:, c - 1:c, :])).reshape(BLOCK, HG_W))
        else:
            i = 1 + l - (N_LEVELS - N_MXU_LEVELS)
            e_lv.append(e_mxu[i * BLOCK:(i + 1) * BLOCK])
    return dict(
        q=q, kk=kk, gate=pg * _sigmoid(pg), vb=pi.astype(jnp.bfloat16), e_lv=e_lv,
        q_in=(q * jnp.exp2(b_cum)).astype(jnp.bfloat16),
        k_st=(kk * jnp.exp2(b_last - b_cum)).astype(jnp.bfloat16),
        decay_last=jnp.exp2(b_last))


def _hgrn2_levels(pre, hh):
    cols = slice(hh * HG_K, (hh + 1) * HG_K)
    qh = pre["q"][:, cols]
    kh = pre["kk"][:, cols]
    ps = []
    for l in range(N_LEVELS):
        x = jnp.exp2(pre["e_lv"][l][:, cols])
        ps.append(_dot_nt((qh * x).astype(jnp.bfloat16), (kh * x).astype(jnp.bfloat16)))
    ps.append(_dot_nt(qh.astype(jnp.bfloat16), kh.astype(jnp.bfloat16)))
    return ps


def _hgrn2_finish(pre, hh, ps, b, rows, state_scr, ypre_scr, lv_ref, normg_ref):
    cols = slice(hh * HG_K, (hh + 1) * HG_K)
    lv = lv_ref[...]
    vb = pre["vb"][:, cols]
    a = jnp.zeros((BLOCK, BLOCK), jnp.float32)
    for l, p in enumerate(ps):
        a = jnp.where(lv == l, p, a)
    st = state_scr[b, hh]
    o = _dot_nt(pre["q_in"][:, cols], st.astype(jnp.bfloat16)) + _dot(a.astype(jnp.bfloat16), vb)
    state_scr[b, hh] = st * pre["decay_last"][:, cols] + _dot_tn(vb, pre["k_st"][:, cols])
    r = lax.rsqrt(jnp.mean(o * o, axis=-1, keepdims=True) + EPS)
    ypre_scr[rows, cols] = (o * r * normg_ref[:, cols] * pre["gate"][:, cols]).astype(jnp.bfloat16)


def _attn_pre(b, rows, chunk, patt_scr, kband_scr, vband_scr, cos_ref, sin_ref, bias_ref):
    cos = cos_ref[...]
    sin = sin_ref[...]
    q = _rope(patt_scr[rows, 0:ATT_QW], cos, sin) * (HEAD_DIM ** -0.5 * LOG2E)
    k = _rope(patt_scr[rows, ATT_QW:ATT_QW + ATT_KVW], cos, sin)
    v = patt_scr[rows, ATT_QW + ATT_KVW:ATT_QW + 2 * ATT_KVW]
    kband = jnp.concatenate([kband_scr[b], k], axis=0)
    vband = jnp.concatenate([vband_scr[b], v], axis=0)
    kband_scr[b] = k
    vband_scr[b] = v
    return dict(qb=q.astype(jnp.bfloat16), kband=kband, vband=vband,
                bias=bias_ref[jnp.minimum(chunk, 1)])


def _attn_scores(pre, g, kkm_ref):
    qb = pre["qb"]
    qs = jnp.concatenate([qb[:, (2 * g) * 128:(2 * g + 1) * 128],
                          qb[:, (2 * g + 1) * 128:(2 * g + 2) * 128]], axis=0)
    s_band = _dot_nt(qs, _lo_hi(pre["kband"], g).astype(jnp.bfloat16))
    s_meta = _dot_nt(qs, kkm_ref[g])
    return s_band, s_meta


def _attn_softmax(pre, g, scores, sinks_ref):
    s_band, s_meta = scores
    bias = pre["bias"]
    lane = lax.broadcasted_iota(jnp.int32, (BLOCK, 2 * HEAD_DIM), 1)
    meta_mask = (lane < N_META, (lane >= HEAD_DIM) & (lane < HEAD_DIM + N_META))
    p_rows, pm_rows, rinv_rows = [], [], []
    for i in range(2):
        e_b, e_m, dinv = [], [], []
        for j in range(2):
            sink = sinks_ref[4 * g + 2 * i + j] * LOG2E
            sb = s_band[i * BLOCK:(i + 1) * BLOCK, j * 2 * BLOCK:(j + 1) * 2 * BLOCK] + bias
            sm = jnp.where(meta_mask[j], s_meta[i * BLOCK:(i + 1) * BLOCK, :], NEG)
            m = jnp.max(jnp.maximum(jnp.maximum(sb[:, :BLOCK], sb[:, BLOCK:]), sm), axis=-1, keepdims=True)
            m = jnp.maximum(m, sink)
            eb = jnp.exp2(sb - m)
            em = jnp.exp2(sm - m)
            d = (jnp.sum(eb[:, :BLOCK] + eb[:, BLOCK:] + em, axis=-1, keepdims=True)
                 + jnp.exp2(sink - m))
            e_b.append(eb.astype(jnp.bfloat16))
            e_m.append(em)
            dinv.append(1.0 / d)
        p_rows.append(jnp.concatenate(e_b, axis=1))
        pm_rows.append((e_m[0] + e_m[1]).astype(jnp.bfloat16))
        rinv_rows.append(jnp.where(lane < HEAD_DIM, dinv[0], dinv[1]))
    return (jnp.concatenate(p_rows, axis=0), jnp.concatenate(pm_rows, axis=0),
            jnp.concatenate(rinv_rows, axis=0))


def _attn_out(pre, g, probs, rows, opre_scr, vvm_ref):
    p, pm, rinv = probs
    o = _dot(p, _lo_hi(pre["vband"], g).astype(jnp.bfloat16)) + _dot(pm, vvm_ref[g])
    o = o * rinv
    opre_scr[rows, (2 * g) * 128:(2 * g + 1) * 128] = o[0:BLOCK].astype(jnp.bfloat16)
    opre_scr[rows, (2 * g + 1) * 128:(2 * g + 2) * 128] = o[BLOCK:2 * BLOCK].astype(jnp.bfloat16)


def _mixer_kernel(x_ref, cos_ref, sin_ref, bias_ref, lv_ref, wcat_ref, lng_ref, lnb_ref,
                  whg_ref, watt_ref, wgate_ref, lb_ref, normg_ref, sinks_ref,
                  wbh_ref, wba_ref, wout_ref, ln1g_ref, ln1b_ref, state0_ref, kkm_ref, vvm_ref,
                  out_ref,
                  state_scr, kband_scr, vband_scr, h_scr, hb_scr, phg_scr, patt_scr, ypre_scr, opre_scr,
                  mixin_scr, res_scr):
    chunk = pl.program_id(1)
    m = BB * BLOCK

    @pl.when(chunk == 0)
    def _():
        for b in range(BB):
            state_scr[b] = state0_ref[...]
        kband_scr[...] = jnp.zeros_like(kband_scr)
        vband_scr[...] = jnp.zeros_like(vband_scr)

    h = _layer_norm(x_ref[...].reshape(m, D_MODEL), lng_ref[...], lnb_ref[...])
    h_scr[...] = h
    hb_scr[...] = h.astype(jnp.bfloat16)
    for k in range(4 * HG_W // DENSE_TN):
        cols = slice(k * DENSE_TN, (k + 1) * DENSE_TN)
        phg_scr[:, cols] = _dot(hb_scr[...], whg_ref[:, cols])
    patt_scr[...] = _dot(hb_scr[...], watt_ref[...])

    def per_group(i, carry):
        heads, groups = [], []
        for bb in range(IB):
            b = i * IB + bb
            rows = pl.ds(pl.multiple_of(b * BLOCK, BLOCK), BLOCK)
            hg = _hgrn2_pre(rows, phg_scr, wcat_ref, lb_ref)
            at = _attn_pre(b, rows, chunk, patt_scr, kband_scr, vband_scr, cos_ref, sin_ref, bias_ref)
            heads.append([(hg, hh, b, rows) for hh in range(HG_HEADS)])
            groups.append([(at, g, rows) for g in range(ATT_KV_HEADS)])
        heads = [u for per_head in zip(*heads) for u in per_head]
        groups = [u for per_group in zip(*groups) for u in per_group]
        levels, scores, probs = {}, {}, {}

        def head_stage1(n):
            hg, hh, _, _ = heads[n]
            levels[n] = _hgrn2_levels(hg, hh)

        def head_stage2(n):
            hg, hh, b, rows = heads[n]
            _hgrn2_finish(hg, hh, levels.pop(n), b, rows, state_scr, ypre_scr, lv_ref, normg_ref)

        def group_stage1(n):
            at, g, _ = groups[n]
            scores[n] = _attn_scores(at, g, kkm_ref)

        def group_stage2(n):
            at, g, _ = groups[n]
            probs[n] = _attn_softmax(at, g, scores.pop(n), sinks_ref)

        def group_stage3(n):
            at, g, rows = groups[n]
            _attn_out(at, g, probs.pop(n), rows, opre_scr, vvm_ref)

        n_groups = len(groups)
        for n in range(n_groups):
            group_stage1(n)
            head_stage1(2 * n)
            if n > 0:
                head_stage2(2 * n - 1)
            group_stage2(n)
            head_stage1(2 * n + 1)
            head_stage2(2 * n)
            group_stage3(n)
        head_stage2(2 * n_groups - 1)
        return carry

    lax.fori_loop(0, BB // IB, per_group, 0)

    for k in range(D_MODEL // DENSE_TN):
        cols = slice(k * DENSE_TN, (k + 1) * DENSE_TN)
        cols2 = slice(D_MODEL + k * DENSE_TN, D_MODEL + (k + 1) * DENSE_TN)
        g_hg = _sigmoid(_dot(hb_scr[...], wgate_ref[:, cols]))
        g_att = _sigmoid(_dot(hb_scr[...], wgate_ref[:, cols2]))
        y_hg = _dot(ypre_scr[...], wbh_ref[:, cols])
        y_att = _dot(opre_scr[...], wba_ref[:, cols])
        mixin_scr[:, cols] = (g_hg * y_hg + g_att * y_att).astype(jnp.bfloat16)
    for k in range(D_MODEL // DENSE_TN):
        cols = slice(k * DENSE_TN, (k + 1) * DENSE_TN)
        res_scr[:, cols] = ALPHA * h_scr[:, cols] + _dot(mixin_scr[...], wout_ref[:, cols])
    out = _layer_norm(res_scr[...], ln1g_ref[...], ln1b_ref[...])
    out_ref[...] = out.reshape(BB, BLOCK, D_MODEL)


def _ffn_kernel(h_ref, wa_ref, wu_ref, wo_ref, g_ref, b_ref, out_ref):
    h = h_ref[...]
    hb = h.astype(jnp.bfloat16)
    acc = jnp.zeros((FFN_TM, D_MODEL), jnp.float32)
    for lo, hi in zip(FFN_SPLITS[:-1], FFN_SPLITS[1:]):
        sl = slice(lo, hi)
        a = _dot(hb, wa_ref[:, sl])
        u = _dot(hb, wu_ref[:, sl])
        hid = (a * _sigmoid(a) * u).astype(jnp.bfloat16)
        acc = acc + _dot(hid, wo_ref[sl, :])
    out_ref[...] = _layer_norm(ALPHA * h + acc, g_ref[...], b_ref[...])


def _const_spec(shape):
    nd = len(shape)
    return pl.BlockSpec(shape, lambda *_: (0,) * nd, pipeline_mode=pl.Buffered(1))


def kernel(x, meta_tokens, ln_emb_g, ln_emb_b, w_in, hg_lower_bounds, hg_norm_g, attn_sinks,
           w_branch_hg, w_branch_attn, w_out, ln1_g, ln1_b, w_ffn_in, w_ffn_out, ln2_g, ln2_b):
    B, S, D = x.shape
    assert D == D_MODEL and S % BLOCK == 0 and B % IB == 0 and (B * S) % FFN_TM == 0
    n_chunks = S // BLOCK
    f32, bf16 = jnp.float32, jnp.bfloat16
    row = lambda v: v.astype(f32).reshape(1, -1)

    lb = jnp.cumsum(jax.nn.softmax(hg_lower_bounds.astype(f32), axis=0), axis=0)[0].reshape(1, HG_W)
    normg = jnp.tile(hg_norm_g[0].astype(f32), HG_HEADS).reshape(1, HG_W)
    w0 = w_in[0].astype(bf16)
    o_att = 4 * HG_W
    o_gate = o_att + ATT_QW + 2 * ATT_KVW
    w_hg, w_att, w_gate = w0[:, :o_att], w0[:, o_att:o_gate], w0[:, o_gate:]
    half = HEAD_DIM // 2
    inv = ROPE_THETA ** (-jnp.arange(half, dtype=f32) / half)
    ang = jnp.arange(S + N_META, dtype=jnp.int32).astype(f32)[:, None] * inv[None, :]
    cos_t = jnp.tile(jnp.cos(ang), (1, 4))
    sin_t = jnp.tile(jnp.concatenate([-jnp.sin(ang), jnp.sin(ang)], axis=1), (1, 2))
    wcat_np, lv_np = _level_tables()
    wcat = jnp.asarray(wcat_np, bf16)
    lv = jnp.asarray(lv_np)
    ucat = jnp.asarray(np.tile(np.triu(np.ones((N_META, N_META), np.float32), 1), (1, 2)), bf16)
    bias = jnp.asarray(_attn_bias())
    cparams = functools.partial(pltpu.CompilerParams, vmem_limit_bytes=VMEM_LIMIT)

    state0, kkm, vvm = pl.pallas_call(
        _meta_kernel,
        out_shape=(jax.ShapeDtypeStruct((HG_HEADS, HG_K, HG_K), f32),
                   jax.ShapeDtypeStruct((ATT_KV_HEADS, BLOCK, 2 * HEAD_DIM), bf16),
                   jax.ShapeDtypeStruct((ATT_KV_HEADS, BLOCK, 2 * HEAD_DIM), bf16)),
        name="meta",
    )(meta_tokens.astype(f32), row(ln_emb_g), row(ln_emb_b),
      w_hg[:, HG_W:2 * HG_W], w_hg[:, 2 * HG_W:3 * HG_W],
      w_att[:, ATT_QW:ATT_QW + ATT_KVW], w_att[:, ATT_QW + ATT_KVW:],
      lb, cos_t[:N_META], sin_t[:N_META], ucat)

    m = BB * BLOCK
    h1 = pl.pallas_call(
        _mixer_kernel,
        grid=(B // BB, n_chunks),
        in_specs=[
            pl.BlockSpec((BB, BLOCK, D), lambda i, c: (i, c, 0)),
            pl.BlockSpec((BLOCK, 2 * HEAD_DIM), lambda i, c: (c, 0)),
            pl.BlockSpec((BLOCK, 2 * HEAD_DIM), lambda i, c: (c, 0)),
            _const_spec(bias.shape), _const_spec(lv.shape), _const_spec(wcat.shape),
            _const_spec((1, D)), _const_spec((1, D)),
            _const_spec(w_hg.shape), _const_spec(w_att.shape), _const_spec(w_gate.shape),
            _const_spec((1, HG_W)), _const_spec((1, HG_W)),
            pl.BlockSpec(memory_space=pltpu.SMEM),
            _const_spec((HG_W, D)), _const_spec((ATT_QW, D)), _const_spec((D, D)),
            _const_spec((1, D)), _const_spec((1, D)),
            _const_spec(state0.shape), _const_spec(kkm.shape), _const_spec(vvm.shape),
        ],
        out_specs=pl.BlockSpec((BB, BLOCK, D), lambda i, c: (i, c, 0)),
        out_shape=jax.ShapeDtypeStruct((B, S, D), f32),
        scratch_shapes=[
            pltpu.VMEM((BB, HG_HEADS, HG_K, HG_K), f32),
            pltpu.VMEM((BB, BLOCK, ATT_KVW), f32),
            pltpu.VMEM((BB, BLOCK, ATT_KVW), f32),
            pltpu.VMEM((m, D), f32),
            pltpu.VMEM((m, D), bf16),
            pltpu.VMEM((m, 4 * HG_W), f32),
            pltpu.VMEM((m, ATT_QW + 2 * ATT_KVW), f32),
            pltpu.VMEM((m, HG_W), bf16),
            pltpu.VMEM((m, ATT_QW), bf16),
            pltpu.VMEM((m, D), bf16),
            pltpu.VMEM((m, D), f32),
        ],
        compiler_params=cparams(dimension_semantics=("arbitrary", "arbitrary")),
        name="mixer",
    )(x.astype(f32), cos_t[N_META:], sin_t[N_META:], bias, lv, wcat, row(ln_emb_g), row(ln_emb_b),
      w_hg, w_att, w_gate, lb, normg, attn_sinks[0].astype(f32),
      w_branch_hg[0].astype(bf16), w_branch_attn[0].astype(bf16), w_out[0].astype(bf16),
      row(ln1_g[0]), row(ln1_b[0]), state0, kkm, vvm)

    wf = w_ffn_in[0].astype(bf16)
    out = pl.pallas_call(
        _ffn_kernel,
        grid=(B * S // FFN_TM,),
        in_specs=[
            pl.BlockSpec((FFN_TM, D), lambda i: (i, 0)),
            _const_spec((D, D_FF)), _const_spec((D, D_FF)), _const_spec((D_FF, D)),
            _const_spec((1, D)), _const_spec((1, D)),
        ],
        out_specs=pl.BlockSpec((FFN_TM, D), lambda i: (i, 0)),
        out_shape=jax.ShapeDtypeStruct((B * S, D), f32),
        compiler_params=cparams(dimension_semantics=("arbitrary",)),
        name="ffn",
    )(h1.reshape(B * S, D), wf[:, :D_FF], wf[:, D_FF:], w_ffn_out[0].astype(bf16),
      row(ln2_g[0]), row(ln2_b[0]))
    return out.reshape(B, S, D)
```

```python
import functools

import numpy as np
import jax
import jax.numpy as jnp
from jax import lax
from jax.experimental import pallas as pl
from jax.experimental.pallas import tpu as pltpu

D_MODEL = 1024
N_META = 16
BLOCK = 128
HG_HEADS = 4
HG_K = 128
HG_W = HG_HEADS * HG_K
ATT_HEADS = 8
ATT_KV_HEADS = 2
HEAD_DIM = 64
ATT_QW = ATT_HEADS * HEAD_DIM
ATT_KVW = ATT_KV_HEADS * HEAD_DIM
D_FF = 2816
EPS = 1e-5
DEPTH = 1
ALPHA = (2.0 * DEPTH) ** 0.25
ROPE_THETA = 10000.0
NEG = -1e30
LOG2E = 1.4426950408889634

N_LEVELS = 7
DIAG_ID = N_LEVELS
N_MXU_LEVELS = 2
BB = 4
IB = 2
FFN_TM = 512
MXU_N = 256
SUBLANES = 8
FFN_SPLITS = (0, 6 * MXU_N, D_FF)
DENSE_TN = 512
VMEM_LIMIT = 56 * 1024 * 1024


def _level_tables():
    n = BLOCK
    t = np.arange(n)[:, None]
    j = np.arange(n)[None, :]
    blocks = [(j <= t)]
    lv = np.full((n, n), -1, np.int32)
    s = np.arange(n)[None, :]
    for l in range(N_LEVELS):
        c = n >> (l + 1)
        mid = (t // (2 * c)) * (2 * c) + c - 1
        second = (t % (2 * c)) >= c
        if l >= N_LEVELS - N_MXU_LEVELS:
            blocks.append(np.where(second, (j > mid) & (j <= t), (j > t) & (j <= mid)))
        same = (t // (2 * c)) == (s // (2 * c))
        lv = np.where(same & second & ((s % (2 * c)) < c), l, lv)
    lv = np.where(t == s, DIAG_ID, lv).astype(np.int32)
    w = np.concatenate(blocks, axis=0).astype(np.float32)
    return np.concatenate([w, w], axis=1), lv


def _attn_bias():
    r = np.arange(BLOCK)[:, None]
    c = np.arange(2 * BLOCK)[None, :]
    cur = (c >= BLOCK) & ((c - BLOCK) <= r)
    prev = (c < BLOCK) & (c > r)
    b0 = np.where(cur, 0.0, NEG)
    b1 = np.where(cur | prev, 0.0, NEG)
    return np.stack([b0, b1]).astype(np.float32)


def _sigmoid(x):
    return 1.0 / (1.0 + jnp.exp(-x))


def _layer_norm(x, g, b):
    mu = jnp.mean(x, axis=-1, keepdims=True)
    xc = x - mu
    var = jnp.mean(xc * xc, axis=-1, keepdims=True)
    return xc * lax.rsqrt(var + EPS) * g + b


def _dot(a, b):
    return jnp.dot(a, b, preferred_element_type=jnp.float32)


def _dot_nt(a, b):
    return lax.dot_general(a, b, (((1,), (1,)), ((), ())), preferred_element_type=jnp.float32)


def _dot_tn(a, b):
    return lax.dot_general(a, b, (((0,), (0,)), ((), ())), preferred_element_type=jnp.float32)


def _split2(x):
    hi = x.astype(jnp.bfloat16)
    lo = (x - hi.astype(jnp.float32)).astype(jnp.bfloat16)
    return jnp.concatenate([hi, lo], axis=0)


def _rope(x, cos, sin):
    w = x.shape[1]
    lane = lax.broadcasted_iota(jnp.int32, x.shape, 1)
    partner = jnp.where((lane & (HEAD_DIM - 1)) < HEAD_DIM // 2,
                        pltpu.roll(x, w - HEAD_DIM // 2, 1), pltpu.roll(x, HEAD_DIM // 2, 1))
    reps = w // cos.shape[1]
    if reps > 1:
        cos = jnp.concatenate([cos] * reps, axis=1)
        sin = jnp.concatenate([sin] * reps, axis=1)
    return x * cos + partner * sin


def _lo_hi(a, group):
    lane = lax.broadcasted_iota(jnp.int32, a.shape, 1)
    swapped = pltpu.roll(a, HEAD_DIM, 1)
    zero = jnp.zeros_like(a)
    if group == 0:
        lo = jnp.where(lane < HEAD_DIM, a, zero)
        hi = jnp.where(lane >= HEAD_DIM, swapped, zero)
    else:
        lo = jnp.where(lane < HEAD_DIM, swapped, zero)
        hi = jnp.where(lane >= HEAD_DIM, a, zero)
    return jnp.concatenate([lo, hi], axis=0)


def _meta_kernel(meta_ref, g_ref, b_ref, wf_ref, wi_ref, wk_ref, wv_ref, lb_ref, cos_ref, sin_ref,
                 ucat_ref, state_ref, kkm_ref, vvm_ref):
    h = _layer_norm(meta_ref[...], g_ref[...], b_ref[...])
    hb = h.astype(jnp.bfloat16)
    lb = lb_ref[...]
    fg = lb + (1.0 - lb) * _sigmoid(_dot(hb, wf_ref[...]))
    log2_f = jnp.log2(fg)
    kk = 1.0 - fg
    vb = _dot(hb, wi_ref[...]).astype(jnp.bfloat16)
    suffix = _dot(ucat_ref[...], _split2(log2_f))
    ks = (kk * jnp.exp2(suffix)).astype(jnp.bfloat16)
    for hh in range(HG_HEADS):
        cols = slice(hh * HG_K, (hh + 1) * HG_K)
        state_ref[hh] = _dot_tn(vb[:, cols], ks[:, cols])
    k_rot = _rope(_dot(hb, wk_ref[...]), cos_ref[...], sin_ref[...])
    v = _dot(hb, wv_ref[...])
    pad = jnp.zeros((HEAD_DIM - N_META, 2 * HEAD_DIM), jnp.float32)
    for g in range(ATT_KV_HEADS):
        for src, dst in ((k_rot, kkm_ref), (v, vvm_ref)):
            lh = _lo_hi(src, g)
            dst[g] = jnp.concatenate([lh[:N_META], pad, lh[N_META:], pad], axis=0).astype(jnp.bfloat16)


def _hgrn2_pre(rows, phg_scr, wcat_ref, lb_ref):
    pq = phg_scr[rows, 0 * HG_W:1 * HG_W]
    pf = phg_scr[rows, 1 * HG_W:2 * HG_W]
    pi = phg_scr[rows, 2 * HG_W:3 * HG_W]
    pg = phg_scr[rows, 3 * HG_W:4 * HG_W]
    lb = lb_ref[...]
    q = pq * _sigmoid(pq)
    fg = lb + (1.0 - lb) * _sigmoid(pf)
    log2_f = jnp.log2(fg)
    kk = 1.0 - fg
    e_mxu = _dot(wcat_ref[...], _split2(log2_f))
    b_cum = e_mxu[0:BLOCK]
    b_last = b_cum[BLOCK - 1:BLOCK, :]
    e_lv = []
    for l in range(N_LEVELS):
        c = BLOCK >> (l + 1)
        if l < N_LEVELS - N_MXU_LEVELS:
            b3 = b_cum.reshape(BLOCK // (2 * c), 2 * c, HG_W)
            b_mid = b3[:, c - 1:c, :]
            if c >= SUBLANES:
                e3 = jnp.concatenate([b_mid - b3[:, :c], b3[:, c:] - b_mid], axis=1)
            else:
                e3 = -jnp.abs(b3 - b_mid)
            e_lv.append(e3.reshape(BLOCK, HG_W))
        else:
            i = 1 + l - (N_LEVELS - N_MXU_LEVELS)
            e_lv.append(e_mxu[i * BLOCK:(i + 1) * BLOCK])
    return dict(
        q=q, kk=kk, gate=pg * _sigmoid(pg), vb=pi.astype(jnp.bfloat16), e_lv=e_lv,
        q_in=(q * jnp.exp2(b_cum)).astype(jnp.bfloat16),
        k_st=(kk * jnp.exp2(b_last - b_cum)).astype(jnp.bfloat16),
        decay_last=jnp.exp2(b_last))


def _hgrn2_levels(pre, hh):
    cols = slice(hh * HG_K, (hh + 1) * HG_K)
    qh = pre["q"][:, cols]
    kh = pre["kk"][:, cols]
    ps = []
    for l in range(N_LEVELS):
        x = jnp.exp2(pre["e_lv"][l][:, cols])
        c = BLOCK >> (l + 1)
        if c >= SUBLANES:
            q3 = qh.reshape(BLOCK // (2 * c), 2 * c, HG_K)
            k3 = kh.reshape(BLOCK // (2 * c), 2 * c, HG_K)
            z = jnp.concatenate([k3[:, :c], q3[:, c:]], axis=1).reshape(BLOCK, HG_K)
            zb = (z * x).astype(jnp.bfloat16)
            ps.append(_dot_nt(zb, zb))
        else:
            ps.append(_dot_nt((qh * x).astype(jnp.bfloat16), (kh * x).astype(jnp.bfloat16)))
    ps.append(_dot_nt(qh.astype(jnp.bfloat16), kh.astype(jnp.bfloat16)))
    return ps


def _hgrn2_finish(pre, hh, ps, b, rows, state_scr, ypre_scr, lv_ref, normg_ref):
    cols = slice(hh * HG_K, (hh + 1) * HG_K)
    lv = lv_ref[...]
    vb = pre["vb"][:, cols]
    a = jnp.zeros((BLOCK, BLOCK), jnp.float32)
    for l, p in enumerate(ps):
        a = jnp.where(lv == l, p, a)
    st = state_scr[b, hh]
    o = _dot_nt(pre["q_in"][:, cols], st.astype(jnp.bfloat16)) + _dot(a.astype(jnp.bfloat16), vb)
    state_scr[b, hh] = st * pre["decay_last"][:, cols] + _dot_tn(vb, pre["k_st"][:, cols])
    r = lax.rsqrt(jnp.mean(o * o, axis=-1, keepdims=True) + EPS)
    ypre_scr[rows, cols] = (o * r * normg_ref[:, cols] * pre["gate"][:, cols]).astype(jnp.bfloat16)


def _attn_pre(b, rows, chunk, patt_scr, kband_scr, vband_scr, cos_ref, sin_ref, bias_ref):
    cos = cos_ref[...]
    sin = sin_ref[...]
    q = _rope(patt_scr[rows, 0:ATT_QW], cos, sin) * (HEAD_DIM ** -0.5 * LOG2E)
    k = _rope(patt_scr[rows, ATT_QW:ATT_QW + ATT_KVW], cos, sin)
    v = patt_scr[rows, ATT_QW + ATT_KVW:ATT_QW + 2 * ATT_KVW]
    kband = jnp.concatenate([kband_scr[b], k], axis=0)
    vband = jnp.concatenate([vband_scr[b], v], axis=0)
    kband_scr[b] = k
    vband_scr[b] = v
    return dict(qb=q.astype(jnp.bfloat16), kband=kband, vband=vband,
                bias=bias_ref[jnp.minimum(chunk, 1)])


def _attn_scores(pre, g, kkm_ref):
    qb = pre["qb"]
    qs = jnp.concatenate([qb[:, (2 * g) * 128:(2 * g + 1) * 128],
                          qb[:, (2 * g + 1) * 128:(2 * g + 2) * 128]], axis=0)
    s_band = _dot_nt(qs, _lo_hi(pre["kband"], g).astype(jnp.bfloat16))
    s_meta = _dot_nt(qs, kkm_ref[g])
    return s_band, s_meta


def _attn_softmax(pre, g, scores, sinks_ref):
    s_band, s_meta = scores
    bias = pre["bias"]
    lane = lax.broadcasted_iota(jnp.int32, (BLOCK, 2 * HEAD_DIM), 1)
    meta_mask = (lane < N_META, (lane >= HEAD_DIM) & (lane < HEAD_DIM + N_META))
    p_rows, pm_rows, rinv_rows = [], [], []
    for i in range(2):
        e_b, e_m, dinv = [], [], []
        for j in range(2):
            sink = sinks_ref[4 * g + 2 * i + j] * LOG2E
            sb = s_band[i * BLOCK:(i + 1) * BLOCK, j * 2 * BLOCK:(j + 1) * 2 * BLOCK] + bias
            sm = jnp.where(meta_mask[j], s_meta[i * BLOCK:(i + 1) * BLOCK, :], NEG)
            m = jnp.max(jnp.maximum(jnp.maximum(sb[:, :BLOCK], sb[:, BLOCK:]), sm), axis=-1, keepdims=True)
            m = jnp.maximum(m, sink)
            eb = jnp.exp2(sb - m)
            em = jnp.exp2(sm - m)
            d = (jnp.sum(eb[:, :BLOCK] + eb[:, BLOCK:] + em, axis=-1, keepdims=True)
                 + jnp.exp2(sink - m))
            e_b.append(eb.astype(jnp.bfloat16))
            e_m.append(em)
            dinv.append(1.0 / d)
        p_rows.append(jnp.concatenate(e_b, axis=1))
        pm_rows.append((e_m[0] + e_m[1]).astype(jnp.bfloat16))
        rinv_rows.append(jnp.where(lane < HEAD_DIM, dinv[0], dinv[1]))
    return (jnp.concatenate(p_rows, axis=0), jnp.concatenate(pm_rows, axis=0),
            jnp.concatenate(rinv_rows, axis=0))


def _attn_out(pre, g, probs, rows, opre_scr, vvm_ref):
    p, pm, rinv = probs
    o = _dot(p, _lo_hi(pre["vband"], g).astype(jnp.bfloat16)) + _dot(pm, vvm_ref[g])
    o = o * rinv
    opre_scr[rows, (2 * g) * 128:(2 * g + 1) * 128] = o[0:BLOCK].astype(jnp.bfloat16)
    opre_scr[rows, (2 * g + 1) * 128:(2 * g + 2) * 128] = o[BLOCK:2 * BLOCK].astype(jnp.bfloat16)


def _mixer_kernel(x_ref, x0_ref, cos_ref, sin_ref, bias_ref, lv_ref, wcat_ref, lng_ref, lnb_ref,
                  whg_ref, watt_ref, wgate_ref, lb_ref, normg_ref, sinks_ref,
                  wbh_ref, wba_ref, wout_ref, ln1g_ref, ln1b_ref, state0_ref, kkm_ref, vvm_ref,
                  out_ref,
                  state_scr, kband_scr, vband_scr, h_scr, hb_scr, phg_scr, patt_scr, ypre_scr, opre_scr,
                  mixin_scr, res_scr):
    chunk = pl.program_id(1)
    m = BB * BLOCK

    def embed(x_blocks):
        return _layer_norm(x_blocks.reshape(m, D_MODEL), lng_ref[...], lnb_ref[...])

    def input_proj(hb, blocks):
        for k in blocks:
            if k < 4 * HG_W // DENSE_TN:
                cols = slice(k * DENSE_TN, (k + 1) * DENSE_TN)
                phg_scr[:, cols] = _dot(hb, whg_ref[:, cols])
            else:
                patt_scr[...] = _dot(hb, watt_ref[...])

    n_proj = 4 * HG_W // DENSE_TN + 1

    @pl.when(chunk == 0)
    def _():
        for b in range(BB):
            state_scr[b] = state0_ref[...]
        kband_scr[...] = jnp.zeros_like(kband_scr)
        vband_scr[...] = jnp.zeros_like(vband_scr)
        h0 = embed(x0_ref[...])
        h_scr[...] = h0
        hb_scr[...] = h0.astype(jnp.bfloat16)
        input_proj(hb_scr[...], range(n_proj))

    def per_group(i, carry):
        heads, groups = [], []
        for bb in range(IB):
            b = i * IB + bb
            rows = pl.ds(pl.multiple_of(b * BLOCK, BLOCK), BLOCK)
            hg = _hgrn2_pre(rows, phg_scr, wcat_ref, lb_ref)
            at = _attn_pre(b, rows, chunk, patt_scr, kband_scr, vband_scr, cos_ref, sin_ref, bias_ref)
            heads.append([(hg, hh, b, rows) for hh in range(HG_HEADS)])
            groups.append([(at, g, rows) for g in range(ATT_KV_HEADS)])
        heads = [u for per_head in zip(*heads) for u in per_head]
        groups = [u for per_group in zip(*groups) for u in per_group]
        levels, scores, probs = {}, {}, {}

        def head_stage1(n):
            hg, hh, _, _ = heads[n]
            levels[n] = _hgrn2_levels(hg, hh)

        def head_stage2(n):
            hg, hh, b, rows = heads[n]
            _hgrn2_finish(hg, hh, levels.pop(n), b, rows, state_scr, ypre_scr, lv_ref, normg_ref)

        def group_stage1(n):
            at, g, _ = groups[n]
            scores[n] = _attn_scores(at, g, kkm_ref)

        def group_stage2(n):
            at, g, _ = groups[n]
            probs[n] = _attn_softmax(at, g, scores.pop(n), sinks_ref)

        def group_stage3(n):
            at, g, rows = groups[n]
            _attn_out(at, g, probs.pop(n), rows, opre_scr, vvm_ref)

        n_groups = len(groups)
        for n in range(n_groups):
            group_stage1(n)
            head_stage1(2 * n)
            if n > 0:
                head_stage2(2 * n - 1)
            group_stage2(n)
            head_stage1(2 * n + 1)
            head_stage2(2 * n)
            group_stage3(n)
        head_stage2(2 * n_groups - 1)
        return carry

    lax.fori_loop(0, BB // IB, per_group, 0)

    h_next = embed(x_ref[...])
    hb_next = h_next.astype(jnp.bfloat16)
    for k in range(D_MODEL // DENSE_TN):
        cols = slice(k * DENSE_TN, (k + 1) * DENSE_TN)
        cols2 = slice(D_MODEL + k * DENSE_TN, D_MODEL + (k + 1) * DENSE_TN)
        g_hg = _sigmoid(_dot(hb_scr[...], wgate_ref[:, cols]))
        g_att = _sigmoid(_dot(hb_scr[...], wgate_ref[:, cols2]))
        y_hg = _dot(ypre_scr[...], wbh_ref[:, cols])
        y_att = _dot(opre_scr[...], wba_ref[:, cols])
        mixin_scr[:, cols] = (g_hg * y_hg + g_att * y_att).astype(jnp.bfloat16)
    for k in range(D_MODEL // DENSE_TN):
        cols = slice(k * DENSE_TN, (k + 1) * DENSE_TN)
        res_scr[:, cols] = ALPHA * h_scr[:, cols] + _dot(mixin_scr[...], wout_ref[:, cols])
    h_scr[...] = h_next
    hb_scr[...] = hb_next
    input_proj(hb_next, range(0, 2))
    out = _layer_norm(res_scr[...], ln1g_ref[...], ln1b_ref[...])
    out_ref[...] = out.reshape(BB, BLOCK, D_MODEL)
    input_proj(hb_next, range(2, n_proj))


def _ffn_kernel(h_ref, wa_ref, wu_ref, wo_ref, g_ref, b_ref, out_ref, res_scr):
    @pl.when(pl.program_id(0) == 0)
    def _():
        res_scr[...] = jnp.zeros_like(res_scr)

    h = h_ref[...]
    hb = h.astype(jnp.bfloat16)
    acc = jnp.zeros((FFN_TM, D_MODEL), jnp.float32)
    for n, (lo, hi) in enumerate(zip(FFN_SPLITS[:-1], FFN_SPLITS[1:])):
        sl = slice(lo, hi)
        a = _dot(hb, wa_ref[:, sl])
        u = _dot(hb, wu_ref[:, sl])
        hid = (a * _sigmoid(a) * u).astype(jnp.bfloat16)
        acc = acc + _dot(hid, wo_ref[sl, :])
        if n == 0:
            out_ref[...] = _layer_norm(res_scr[...], g_ref[...], b_ref[...])
    res_scr[...] = ALPHA * h + acc


def _const_spec(shape):
    nd = len(shape)
    return pl.BlockSpec(shape, lambda *_: (0,) * nd, pipeline_mode=pl.Buffered(1))


def kernel(x, meta_tokens, ln_emb_g, ln_emb_b, w_in, hg_lower_bounds, hg_norm_g, attn_sinks,
           w_branch_hg, w_branch_attn, w_out, ln1_g, ln1_b, w_ffn_in, w_ffn_out, ln2_g, ln2_b):
    B, S, D = x.shape
    assert D == D_MODEL and S % BLOCK == 0 and B % IB == 0 and (B * S) % FFN_TM == 0
    n_chunks = S // BLOCK
    f32, bf16 = jnp.float32, jnp.bfloat16
    row = lambda v: v.astype(f32).reshape(1, -1)

    lb = jnp.cumsum(jax.nn.softmax(hg_lower_bounds.astype(f32), axis=0), axis=0)[0].reshape(1, HG_W)
    normg = jnp.tile(hg_norm_g[0].astype(f32), HG_HEADS).reshape(1, HG_W)
    w0 = w_in[0].astype(bf16)
    o_att = 4 * HG_W
    o_gate = o_att + ATT_QW + 2 * ATT_KVW
    w_hg, w_att, w_gate = w0[:, :o_att], w0[:, o_att:o_gate], w0[:, o_gate:]
    half = HEAD_DIM // 2
    inv = ROPE_THETA ** (-jnp.arange(half, dtype=f32) / half)
    ang = jnp.arange(S + N_META, dtype=jnp.int32).astype(f32)[:, None] * inv[None, :]
    cos_t = jnp.tile(jnp.cos(ang), (1, 4))
    sin_t = jnp.tile(jnp.concatenate([-jnp.sin(ang), jnp.sin(ang)], axis=1), (1, 2))
    wcat_np, lv_np = _level_tables()
    wcat = jnp.asarray(wcat_np, bf16)
    lv = jnp.asarray(lv_np)
    ucat = jnp.asarray(np.tile(np.triu(np.ones((N_META, N_META), np.float32), 1), (1, 2)), bf16)
    bias = jnp.asarray(_attn_bias())
    cparams = functools.partial(pltpu.CompilerParams, vmem_limit_bytes=VMEM_LIMIT)

    state0, kkm, vvm = pl.pallas_call(
        _meta_kernel,
        out_shape=(jax.ShapeDtypeStruct((HG_HEADS, HG_K, HG_K), f32),
                   jax.ShapeDtypeStruct((ATT_KV_HEADS, BLOCK, 2 * HEAD_DIM), bf16),
                   jax.ShapeDtypeStruct((ATT_KV_HEADS, BLOCK, 2 * HEAD_DIM), bf16)),
        name="meta",
    )(meta_tokens.astype(f32), row(ln_emb_g), row(ln_emb_b),
      w_hg[:, HG_W:2 * HG_W], w_hg[:, 2 * HG_W:3 * HG_W],
      w_att[:, ATT_QW:ATT_QW + ATT_KVW], w_att[:, ATT_QW + ATT_KVW:],
      lb, cos_t[:N_META], sin_t[:N_META], ucat)

    m = BB * BLOCK
    h1 = pl.pallas_call(
        _mixer_kernel,
        grid=(B // BB, n_chunks),
        in_specs=[
            pl.BlockSpec((BB, BLOCK, D), lambda i, c: (i, jnp.minimum(c + 1, n_chunks - 1), 0)),
            pl.BlockSpec((BB, BLOCK, D), lambda i, c: (i, 0, 0)),
            pl.BlockSpec((BLOCK, 2 * HEAD_DIM), lambda i, c: (c, 0)),
            pl.BlockSpec((BLOCK, 2 * HEAD_DIM), lambda i, c: (c, 0)),
            _const_spec(bias.shape), _const_spec(lv.shape), _const_spec(wcat.shape),
            _const_spec((1, D)), _const_spec((1, D)),
            _const_spec(w_hg.shape), _const_spec(w_att.shape), _const_spec(w_gate.shape),
            _const_spec((1, HG_W)), _const_spec((1, HG_W)),
            pl.BlockSpec(memory_space=pltpu.SMEM),
            _const_spec((HG_W, D)), _const_spec((ATT_QW, D)), _const_spec((D, D)),
            _const_spec((1, D)), _const_spec((1, D)),
            _const_spec(state0.shape), _const_spec(kkm.shape), _const_spec(vvm.shape),
        ],
        out_specs=pl.BlockSpec((BB, BLOCK, D), lambda i, c: (i, c, 0)),
        out_shape=jax.ShapeDtypeStruct((B, S, D), f32),
        scratch_shapes=[
            pltpu.VMEM((BB, HG_HEADS, HG_K, HG_K), f32),
            pltpu.VMEM((BB, BLOCK, ATT_KVW), f32),
            pltpu.VMEM((BB, BLOCK, ATT_KVW), f32),
            pltpu.VMEM((m, D), f32),
            pltpu.VMEM((m, D), bf16),
            pltpu.VMEM((m, 4 * HG_W), f32),
            pltpu.VMEM((m, ATT_QW + 2 * ATT_KVW), f32),
            pltpu.VMEM((m, HG_W), bf16),
            pltpu.VMEM((m, ATT_QW), bf16),
            pltpu.VMEM((m, D), bf16),
            pltpu.VMEM((m, D), f32),
        ],
        compiler_params=cparams(dimension_semantics=("arbitrary", "arbitrary")),
        name="mixer",
    )(x.astype(f32), x.astype(f32), cos_t[N_META:], sin_t[N_META:], bias, lv, wcat,
      row(ln_emb_g), row(ln_emb_b),
      w_hg, w_att, w_gate, lb, normg, attn_sinks[0].astype(f32),
      w_branch_hg[0].astype(bf16), w_branch_attn[0].astype(bf16), w_out[0].astype(bf16),
      row(ln1_g[0]), row(ln1_b[0]), state0, kkm, vvm)

    wf = w_ffn_in[0].astype(bf16)
    n_tiles = B * S // FFN_TM
    out = pl.pallas_call(
        _ffn_kernel,
        grid=(n_tiles + 1,),
        in_specs=[
            pl.BlockSpec((FFN_TM, D), lambda i: (jnp.minimum(i, n_tiles - 1), 0)),
            _const_spec((D, D_FF)), _const_spec((D, D_FF)), _const_spec((D_FF, D)),
            _const_spec((1, D)), _const_spec((1, D)),
        ],
        out_specs=pl.BlockSpec((FFN_TM, D), lambda i: (jnp.maximum(i - 1, 0), 0)),
        out_shape=jax.ShapeDtypeStruct((B * S, D), f32),
        scratch_shapes=[pltpu.VMEM((FFN_TM, D), f32)],
        compiler_params=cparams(dimension_semantics=("arbitrary",)),
        name="ffn",
    )(h1.reshape(B * S, D), wf[:, :D_FF], wf[:, D_FF:], w_ffn_out[0].astype(bf16),
      row(ln2_g[0]), row(ln2_b[0]))
    return out.reshape(B, S, D)
```

```python
import functools

import numpy as np
import jax
import jax.numpy as jnp
from jax import lax
from jax.experimental import pallas as pl
from jax.experimental.pallas import tpu as pltpu

D_MODEL = 1024
N_META = 16
BLOCK = 128
HG_HEADS = 4
HG_K = 128
HG_W = HG_HEADS * HG_K
ATT_HEADS = 8
ATT_KV_HEADS = 2
HEAD_DIM = 64
ATT_QW = ATT_HEADS * HEAD_DIM
ATT_KVW = ATT_KV_HEADS * HEAD_DIM
D_FF = 2816
EPS = 1e-5
DEPTH = 1
ALPHA = (2.0 * DEPTH) ** 0.25
ROPE_THETA = 10000.0
NEG = -1e30
LOG2E = 1.4426950408889634

N_LEVELS = 7
DIAG_ID = N_LEVELS
N_MXU_LEVELS = 2
BB = 4
FFN_TM = 512
MXU_N = 256
SUBLANES = 8
FFN_SPLITS = (0, 6 * MXU_N, D_FF)
DENSE_TN = 512
VMEM_LIMIT = 56 * 1024 * 1024


def _level_tables():
    n = BLOCK
    t = np.arange(n)[:, None]
    j = np.arange(n)[None, :]
    blocks = [(j <= t)]
    lv = np.full((n, n), -1, np.int32)
    s = np.arange(n)[None, :]
    for l in range(N_LEVELS):
        c = n >> (l + 1)
        mid = (t // (2 * c)) * (2 * c) + c - 1
        second = (t % (2 * c)) >= c
        if l >= N_LEVELS - N_MXU_LEVELS:
            blocks.append(np.where(second, (j > mid) & (j <= t), (j > t) & (j <= mid)))
        same = (t // (2 * c)) == (s // (2 * c))
        lv = np.where(same & second & ((s % (2 * c)) < c), l, lv)
    lv = np.where(t == s, DIAG_ID, lv).astype(np.int32)
    w = np.concatenate(blocks, axis=0).astype(np.float32)
    return np.concatenate([w, w], axis=1), lv


def _attn_bias():
    r = np.arange(BLOCK)[:, None]
    c = np.arange(2 * BLOCK)[None, :]
    cur = (c >= BLOCK) & ((c - BLOCK) <= r)
    prev = (c < BLOCK) & (c > r)
    b0 = np.where(cur, 0.0, NEG)
    b1 = np.where(cur | prev, 0.0, NEG)
    return np.stack([b0, b1]).astype(np.float32)


def _sigmoid(x):
    return 1.0 / (1.0 + jnp.exp(-x))


def _layer_norm(x, g, b):
    mu = jnp.mean(x, axis=-1, keepdims=True)
    xc = x - mu
    var = jnp.mean(xc * xc, axis=-1, keepdims=True)
    return xc * lax.rsqrt(var + EPS) * g + b


def _dot(a, b):
    return jnp.dot(a, b, preferred_element_type=jnp.float32)


def _dot_nt(a, b):
    return lax.dot_general(a, b, (((1,), (1,)), ((), ())), preferred_element_type=jnp.float32)


def _dot_tn(a, b):
    return lax.dot_general(a, b, (((0,), (0,)), ((), ())), preferred_element_type=jnp.float32)


def _split2(x):
    hi = x.astype(jnp.bfloat16)
    lo = (x - hi.astype(jnp.float32)).astype(jnp.bfloat16)
    return jnp.concatenate([hi, lo], axis=0)


def _rope(x, cos, sin):
    w = x.shape[1]
    lane = lax.broadcasted_iota(jnp.int32, x.shape, 1)
    partner = jnp.where((lane & (HEAD_DIM - 1)) < HEAD_DIM // 2,
                        pltpu.roll(x, w - HEAD_DIM // 2, 1), pltpu.roll(x, HEAD_DIM // 2, 1))
    reps = w // cos.shape[1]
    if reps > 1:
        cos = jnp.concatenate([cos] * reps, axis=1)
        sin = jnp.concatenate([sin] * reps, axis=1)
    return x * cos + partner * sin


def _lo_hi(a, group):
    lane = lax.broadcasted_iota(jnp.int32, a.shape, 1)
    swapped = pltpu.roll(a, HEAD_DIM, 1)
    zero = jnp.zeros_like(a)
    if group == 0:
        lo = jnp.where(lane < HEAD_DIM, a, zero)
        hi = jnp.where(lane >= HEAD_DIM, swapped, zero)
    else:
        lo = jnp.where(lane < HEAD_DIM, swapped, zero)
        hi = jnp.where(lane >= HEAD_DIM, a, zero)
    return jnp.concatenate([lo, hi], axis=0)


def _meta_kernel(meta_ref, g_ref, b_ref, wf_ref, wi_ref, wk_ref, wv_ref, lb_ref, cos_ref, sin_ref,
                 ucat_ref, state_ref, kkm_ref, vvm_ref):
    h = _layer_norm(meta_ref[...], g_ref[...], b_ref[...])
    hb = h.astype(jnp.bfloat16)
    lb = lb_ref[...]
    fg = lb + (1.0 - lb) * _sigmoid(_dot(hb, wf_ref[...]))
    log2_f = jnp.log2(fg)
    kk = 1.0 - fg
    vb = _dot(hb, wi_ref[...]).astype(jnp.bfloat16)
    suffix = _dot(ucat_ref[...], _split2(log2_f))
    ks = (kk * jnp.exp2(suffix)).astype(jnp.bfloat16)
    for hh in range(HG_HEADS):
        cols = slice(hh * HG_K, (hh + 1) * HG_K)
        state_ref[hh] = _dot_tn(vb[:, cols], ks[:, cols])
    k_rot = _rope(_dot(hb, wk_ref[...]), cos_ref[...], sin_ref[...])
    v = _dot(hb, wv_ref[...])
    pad = jnp.zeros((HEAD_DIM - N_META, 2 * HEAD_DIM), jnp.float32)
    for g in range(ATT_KV_HEADS):
        for src, dst in ((k_rot, kkm_ref), (v, vvm_ref)):
            lh = _lo_hi(src, g)
            dst[g] = jnp.concatenate([lh[:N_META], pad, lh[N_META:], pad], axis=0).astype(jnp.bfloat16)


def _hgrn2_pre(rows, phg_scr, wcat_ref, lb_ref):
    pq = phg_scr[rows, 0 * HG_W:1 * HG_W]
    pf = phg_scr[rows, 1 * HG_W:2 * HG_W]
    pi = phg_scr[rows, 2 * HG_W:3 * HG_W]
    pg = phg_scr[rows, 3 * HG_W:4 * HG_W]
    lb = lb_ref[...]
    q = pq * _sigmoid(pq)
    fg = lb + (1.0 - lb) * _sigmoid(pf)
    log2_f = jnp.log2(fg)
    kk = 1.0 - fg
    e_mxu = _dot(wcat_ref[...], _split2(log2_f))
    b_cum = e_mxu[0:BLOCK]
    b_last = b_cum[BLOCK - 1:BLOCK, :]
    e_lv = []
    for l in range(N_LEVELS):
        c = BLOCK >> (l + 1)
        if l < N_LEVELS - N_MXU_LEVELS:
            b3 = b_cum.reshape(BLOCK // (2 * c), 2 * c, HG_W)
            b_mid = b3[:, c - 1:c, :]
            if c >= SUBLANES:
                e3 = jnp.concatenate([b_mid - b3[:, :c], b3[:, c:] - b_mid], axis=1)
            else:
                e3 = -jnp.abs(b3 - b_mid)
            e_lv.append(e3.reshape(BLOCK, HG_W))
        else:
            i = 1 + l - (N_LEVELS - N_MXU_LEVELS)
            e_lv.append(e_mxu[i * BLOCK:(i + 1) * BLOCK])
    return dict(
        q=q, kk=kk, gate=pg * _sigmoid(pg), vb=pi.astype(jnp.bfloat16), e_lv=e_lv,
        q_in=(q * jnp.exp2(b_cum)).astype(jnp.bfloat16),
        k_st=(kk * jnp.exp2(b_last - b_cum)).astype(jnp.bfloat16),
        decay_last=jnp.exp2(b_last))


def _hgrn2_levels(pre, hh):
    cols = slice(hh * HG_K, (hh + 1) * HG_K)
    qh = pre["q"][:, cols]
    kh = pre["kk"][:, cols]
    ps = []
    for l in range(N_LEVELS):
        x = jnp.exp2(pre["e_lv"][l][:, cols])
        c = BLOCK >> (l + 1)
        if c >= SUBLANES:
            q3 = qh.reshape(BLOCK // (2 * c), 2 * c, HG_K)
            k3 = kh.reshape(BLOCK // (2 * c), 2 * c, HG_K)
            z = jnp.concatenate([k3[:, :c], q3[:, c:]], axis=1).reshape(BLOCK, HG_K)
            zb = (z * x).astype(jnp.bfloat16)
            ps.append(_dot_nt(zb, zb))
        else:
            ps.append(_dot_nt((qh * x).astype(jnp.bfloat16), (kh * x).astype(jnp.bfloat16)))
    ps.append(_dot_nt(qh.astype(jnp.bfloat16), kh.astype(jnp.bfloat16)))
    return ps


def _hgrn2_finish(pre, hh, ps, b, rows, state_scr, ypre_scr, lv_ref, normg_ref):
    cols = slice(hh * HG_K, (hh + 1) * HG_K)
    lv = lv_ref[...]
    vb = pre["vb"][:, cols]
    a = jnp.zeros((BLOCK, BLOCK), jnp.float32)
    for l, p in enumerate(ps):
        a = jnp.where(lv == l, p, a)
    st = state_scr[b, hh]
    o = _dot_nt(pre["q_in"][:, cols], st.astype(jnp.bfloat16)) + _dot(a.astype(jnp.bfloat16), vb)
    state_scr[b, hh] = st * pre["decay_last"][:, cols] + _dot_tn(vb, pre["k_st"][:, cols])
    r = lax.rsqrt(jnp.mean(o * o, axis=-1, keepdims=True) + EPS)
    ypre_scr[rows, cols] = (o * r * normg_ref[:, cols] * pre["gate"][:, cols]).astype(jnp.bfloat16)


def _attn_pre(b, rows, chunk, patt_scr, kband_scr, vband_scr, cos_ref, sin_ref, bias_ref):
    cos = cos_ref[...]
    sin = sin_ref[...]
    q = _rope(patt_scr[rows, 0:ATT_QW], cos, sin) * (HEAD_DIM ** -0.5 * LOG2E)
    k = _rope(patt_scr[rows, ATT_QW:ATT_QW + ATT_KVW], cos, sin)
    v = patt_scr[rows, ATT_QW + ATT_KVW:ATT_QW + 2 * ATT_KVW]
    kband = jnp.concatenate([kband_scr[b], k], axis=0)
    vband = jnp.concatenate([vband_scr[b], v], axis=0)
    kband_scr[b] = k
    vband_scr[b] = v
    return dict(qb=q.astype(jnp.bfloat16), kband=kband, vband=vband,
                bias=bias_ref[jnp.minimum(chunk, 1)])


def _attn_scores(pre, g, kkm_ref):
    qb = pre["qb"]
    qs = jnp.concatenate([qb[:, (2 * g) * 128:(2 * g + 1) * 128],
                          qb[:, (2 * g + 1) * 128:(2 * g + 2) * 128]], axis=0)
    s_band = _dot_nt(qs, _lo_hi(pre["kband"], g).astype(jnp.bfloat16))
    s_meta = _dot_nt(qs, kkm_ref[g])
    return s_band, s_meta


def _attn_softmax(pre, g, scores, sinks_ref):
    s_band, s_meta = scores
    bias = pre["bias"]
    lane = lax.broadcasted_iota(jnp.int32, (BLOCK, 2 * HEAD_DIM), 1)
    meta_mask = (lane < N_META, (lane >= HEAD_DIM) & (lane < HEAD_DIM + N_META))
    p_rows, pm_rows, rinv_rows = [], [], []
    for i in range(2):
        e_b, e_m, dinv = [], [], []
        for j in range(2):
            sink = sinks_ref[4 * g + 2 * i + j] * LOG2E
            sb = s_band[i * BLOCK:(i + 1) * BLOCK, j * 2 * BLOCK:(j + 1) * 2 * BLOCK] + bias
            sm = jnp.where(meta_mask[j], s_meta[i * BLOCK:(i + 1) * BLOCK, :], NEG)
            m = jnp.max(jnp.maximum(jnp.maximum(sb[:, :BLOCK], sb[:, BLOCK:]), sm), axis=-1, keepdims=True)
            m = jnp.maximum(m, sink)
            eb = jnp.exp2(sb - m)
            em = jnp.exp2(sm - m)
            d = (jnp.sum(eb[:, :BLOCK] + eb[:, BLOCK:] + em, axis=-1, keepdims=True)
                 + jnp.exp2(sink - m))
            e_b.append(eb.astype(jnp.bfloat16))
            e_m.append(em)
            dinv.append(1.0 / d)
        p_rows.append(jnp.concatenate(e_b, axis=1))
        pm_rows.append((e_m[0] + e_m[1]).astype(jnp.bfloat16))
        rinv_rows.append(jnp.where(lane < HEAD_DIM, dinv[0], dinv[1]))
    return (jnp.concatenate(p_rows, axis=0), jnp.concatenate(pm_rows, axis=0),
            jnp.concatenate(rinv_rows, axis=0))


def _attn_out(pre, g, probs, rows, opre_scr, vvm_ref):
    p, pm, rinv = probs
    o = _dot(p, _lo_hi(pre["vband"], g).astype(jnp.bfloat16)) + _dot(pm, vvm_ref[g])
    o = o * rinv
    opre_scr[rows, (2 * g) * 128:(2 * g + 1) * 128] = o[0:BLOCK].astype(jnp.bfloat16)
    opre_scr[rows, (2 * g + 1) * 128:(2 * g + 2) * 128] = o[BLOCK:2 * BLOCK].astype(jnp.bfloat16)


def _mixer_kernel(x_ref, x0_ref, cos_ref, sin_ref, bias_ref, lv_ref, wcat_ref, lng_ref, lnb_ref,
                  whg_ref, watt_ref, wgate_ref, lb_ref, normg_ref, sinks_ref,
                  wbh_ref, wba_ref, wout_ref, ln1g_ref, ln1b_ref, state0_ref, kkm_ref, vvm_ref,
                  out_ref,
                  state_scr, kband_scr, vband_scr, h_scr, hb_scr, phg_scr, patt_scr, ypre_scr, opre_scr,
                  mixin_scr, res_scr):
    chunk = pl.program_id(1)
    m = BB * BLOCK

    def embed(x_blocks):
        return _layer_norm(x_blocks.reshape(m, D_MODEL), lng_ref[...], lnb_ref[...])

    n_hg_blocks = 4 * HG_W // MXU_N
    n_proj = n_hg_blocks + (ATT_QW + 2 * ATT_KVW) // MXU_N

    def input_proj(hb, blocks):
        for k in blocks:
            if k < n_hg_blocks:
                cols = slice(k * MXU_N, (k + 1) * MXU_N)
                phg_scr[:, cols] = _dot(hb, whg_ref[:, cols])
            else:
                cols = slice((k - n_hg_blocks) * MXU_N, (k - n_hg_blocks + 1) * MXU_N)
                patt_scr[:, cols] = _dot(hb, watt_ref[:, cols])

    @pl.when(chunk == 0)
    def _():
        for b in range(BB):
            state_scr[b] = state0_ref[...]
        kband_scr[...] = jnp.zeros_like(kband_scr)
        vband_scr[...] = jnp.zeros_like(vband_scr)
        h0 = embed(x0_ref[...])
        h_scr[...] = h0
        hb_scr[...] = h0.astype(jnp.bfloat16)
        input_proj(hb_scr[...], range(n_proj))

    heads, groups = [], []
    for b in range(BB):
        rows = slice(b * BLOCK, (b + 1) * BLOCK)
        hg = _hgrn2_pre(rows, phg_scr, wcat_ref, lb_ref)
        at = _attn_pre(b, rows, chunk, patt_scr, kband_scr, vband_scr, cos_ref, sin_ref, bias_ref)
        heads.append([(hg, hh, b, rows) for hh in range(HG_HEADS)])
        groups.append([(at, g, rows) for g in range(ATT_KV_HEADS)])
    heads = [u for per_head in zip(*heads) for u in per_head]
    groups = [u for per_kv in zip(*groups) for u in per_kv]
    levels, scores, probs = {}, {}, {}

    def head_stage1(n):
        hg, hh, _, _ = heads[n]
        levels[n] = _hgrn2_levels(hg, hh)

    def head_stage2(n):
        hg, hh, b, rows = heads[n]
        _hgrn2_finish(hg, hh, levels.pop(n), b, rows, state_scr, ypre_scr, lv_ref, normg_ref)

    def group_stage1(n):
        at, g, _ = groups[n]
        scores[n] = _attn_scores(at, g, kkm_ref)

    def group_stage2(n):
        at, g, _ = groups[n]
        probs[n] = _attn_softmax(at, g, scores.pop(n), sinks_ref)

    def group_stage3(n):
        at, g, rows = groups[n]
        _attn_out(at, g, probs.pop(n), rows, opre_scr, vvm_ref)

    n_groups = len(groups)
    for n in range(n_groups):
        group_stage1(n)
        head_stage1(2 * n)
        if n > 0:
            head_stage2(2 * n - 1)
        group_stage2(n)
        head_stage1(2 * n + 1)
        head_stage2(2 * n)
        group_stage3(n)
    head_stage2(2 * n_groups - 1)

    h_next = embed(x_ref[...])
    hb_next = h_next.astype(jnp.bfloat16)
    for k in range(D_MODEL // DENSE_TN):
        cols = slice(k * DENSE_TN, (k + 1) * DENSE_TN)
        cols2 = slice(D_MODEL + k * DENSE_TN, D_MODEL + (k + 1) * DENSE_TN)
        g_hg = _sigmoid(_dot(hb_scr[...], wgate_ref[:, cols]))
        g_att = _sigmoid(_dot(hb_scr[...], wgate_ref[:, cols2]))
        y_hg = _dot(ypre_scr[...], wbh_ref[:, cols])
        y_att = _dot(opre_scr[...], wba_ref[:, cols])
        mixin_scr[:, cols] = (g_hg * y_hg + g_att * y_att).astype(jnp.bfloat16)
    for k in range(D_MODEL // DENSE_TN):
        cols = slice(k * DENSE_TN, (k + 1) * DENSE_TN)
        res_scr[:, cols] = ALPHA * h_scr[:, cols] + _dot(mixin_scr[...], wout_ref[:, cols])
    h_scr[...] = h_next
    hb_scr[...] = hb_next
    input_proj(hb_next, range(0, 4))
    out = _layer_norm(res_scr[...], ln1g_ref[...], ln1b_ref[...])
    out_ref[...] = out.reshape(BB, BLOCK, D_MODEL)
    input_proj(hb_next, range(4, n_proj))


def _ffn_kernel(h_ref, wa_ref, wu_ref, wo_ref, g_ref, b_ref, out_ref, res_scr):
    @pl.when(pl.program_id(0) == 0)
    def _():
        res_scr[...] = jnp.zeros_like(res_scr)

    h = h_ref[...]
    hb = h.astype(jnp.bfloat16)
    acc = jnp.zeros((FFN_TM, D_MODEL), jnp.float32)
    for n, (lo, hi) in enumerate(zip(FFN_SPLITS[:-1], FFN_SPLITS[1:])):
        sl = slice(lo, hi)
        a = _dot(hb, wa_ref[:, sl])
        u = _dot(hb, wu_ref[:, sl])
        hid = (a * _sigmoid(a) * u).astype(jnp.bfloat16)
        acc = acc + _dot(hid, wo_ref[sl, :])
        if n == 0:
            out_ref[...] = _layer_norm(res_scr[...], g_ref[...], b_ref[...])
    res_scr[...] = ALPHA * h + acc


def _const_spec(shape):
    nd = len(shape)
    return pl.BlockSpec(shape, lambda *_: (0,) * nd, pipeline_mode=pl.Buffered(1))


def kernel(x, meta_tokens, ln_emb_g, ln_emb_b, w_in, hg_lower_bounds, hg_norm_g, attn_sinks,
           w_branch_hg, w_branch_attn, w_out, ln1_g, ln1_b, w_ffn_in, w_ffn_out, ln2_g, ln2_b):
    B, S, D = x.shape
    assert D == D_MODEL and S % BLOCK == 0 and B % BB == 0 and (B * S) % FFN_TM == 0
    n_chunks = S // BLOCK
    f32, bf16 = jnp.float32, jnp.bfloat16
    row = lambda v: v.astype(f32).reshape(1, -1)

    lb = jnp.cumsum(jax.nn.softmax(hg_lower_bounds.astype(f32), axis=0), axis=0)[0].reshape(1, HG_W)
    normg = jnp.tile(hg_norm_g[0].astype(f32), HG_HEADS).reshape(1, HG_W)
    w0 = w_in[0].astype(bf16)
    o_att = 4 * HG_W
    o_gate = o_att + ATT_QW + 2 * ATT_KVW
    w_hg, w_att, w_gate = w0[:, :o_att], w0[:, o_att:o_gate], w0[:, o_gate:]
    half = HEAD_DIM // 2
    inv = ROPE_THETA ** (-jnp.arange(half, dtype=f32) / half)
    ang = jnp.arange(S + N_META, dtype=jnp.int32).astype(f32)[:, None] * inv[None, :]
    cos_t = jnp.tile(jnp.cos(ang), (1, 4))
    sin_t = jnp.tile(jnp.concatenate([-jnp.sin(ang), jnp.sin(ang)], axis=1), (1, 2))
    wcat_np, lv_np = _level_tables()
    wcat = jnp.asarray(wcat_np, bf16)
    lv = jnp.asarray(lv_np)
    ucat = jnp.asarray(np.tile(np.triu(np.ones((N_META, N_META), np.float32), 1), (1, 2)), bf16)
    bias = jnp.asarray(_attn_bias())
    cparams = functools.partial(pltpu.CompilerParams, vmem_limit_bytes=VMEM_LIMIT)

    state0, kkm, vvm = pl.pallas_call(
        _meta_kernel,
        out_shape=(jax.ShapeDtypeStruct((HG_HEADS, HG_K, HG_K), f32),
                   jax.ShapeDtypeStruct((ATT_KV_HEADS, BLOCK, 2 * HEAD_DIM), bf16),
                   jax.ShapeDtypeStruct((ATT_KV_HEADS, BLOCK, 2 * HEAD_DIM), bf16)),
        name="meta",
    )(meta_tokens.astype(f32), row(ln_emb_g), row(ln_emb_b),
      w_hg[:, HG_W:2 * HG_W], w_hg[:, 2 * HG_W:3 * HG_W],
      w_att[:, ATT_QW:ATT_QW + ATT_KVW], w_att[:, ATT_QW + ATT_KVW:],
      lb, cos_t[:N_META], sin_t[:N_META], ucat)

    m = BB * BLOCK
    h1 = pl.pallas_call(
        _mixer_kernel,
        grid=(B // BB, n_chunks),
        in_specs=[
            pl.BlockSpec((BB, BLOCK, D), lambda i, c: (i, jnp.minimum(c + 1, n_chunks - 1), 0)),
            pl.BlockSpec((BB, BLOCK, D), lambda i, c: (i, 0, 0)),
            pl.BlockSpec((BLOCK, 2 * HEAD_DIM), lambda i, c: (c, 0)),
            pl.BlockSpec((BLOCK, 2 * HEAD_DIM), lambda i, c: (c, 0)),
            _const_spec(bias.shape), _const_spec(lv.shape), _const_spec(wcat.shape),
            _const_spec((1, D)), _const_spec((1, D)),
            _const_spec(w_hg.shape), _const_spec(w_att.shape), _const_spec(w_gate.shape),
            _const_spec((1, HG_W)), _const_spec((1, HG_W)),
            pl.BlockSpec(memory_space=pltpu.SMEM),
            _const_spec((HG_W, D)), _const_spec((ATT_QW, D)), _const_spec((D, D)),
            _const_spec((1, D)), _const_spec((1, D)),
            _const_spec(state0.shape), _const_spec(kkm.shape), _const_spec(vvm.shape),
        ],
        out_specs=pl.BlockSpec((BB, BLOCK, D), lambda i, c: (i, c, 0)),
        out_shape=jax.ShapeDtypeStruct((B, S, D), f32),
        scratch_shapes=[
            pltpu.VMEM((BB, HG_HEADS, HG_K, HG_K), f32),
            pltpu.VMEM((BB, BLOCK, ATT_KVW), f32),
            pltpu.VMEM((BB, BLOCK, ATT_KVW), f32),
            pltpu.VMEM((m, D), f32),
            pltpu.VMEM((m, D), bf16),
            pltpu.VMEM((m, 4 * HG_W), f32),
            pltpu.VMEM((m, ATT_QW + 2 * ATT_KVW), f32),
            pltpu.VMEM((m, HG_W), bf16),
            pltpu.VMEM((m, ATT_QW), bf16),
            pltpu.VMEM((m, D), bf16),
            pltpu.VMEM((m, D), f32),
        ],
        compiler_params=cparams(dimension_semantics=("arbitrary", "arbitrary")),
        name="mixer",
    )(x.astype(f32), x.astype(f32), cos_t[N_META:], sin_t[N_META:], bias, lv, wcat,
      row(ln_emb_g), row(ln_emb_b),
      w_hg, w_att, w_gate, lb, normg, attn_sinks[0].astype(f32),
      w_branch_hg[0].astype(bf16), w_branch_attn[0].astype(bf16), w_out[0].astype(bf16),
      row(ln1_g[0]), row(ln1_b[0]), state0, kkm, vvm)

    wf = w_ffn_in[0].astype(bf16)
    n_tiles = B * S // FFN_TM
    out = pl.pallas_call(
        _ffn_kernel,
        grid=(n_tiles + 1,),
        in_specs=[
            pl.BlockSpec((FFN_TM, D), lambda i: (jnp.minimum(i, n_tiles - 1), 0)),
            _const_spec((D, D_FF)), _const_spec((D, D_FF)), _const_spec((D_FF, D)),
            _const_spec((1, D)), _const_spec((1, D)),
        ],
        out_specs=pl.BlockSpec((FFN_TM, D), lambda i: (jnp.maximum(i - 1, 0), 0)),
        out_shape=jax.ShapeDtypeStruct((B * S, D), f32),
        scratch_shapes=[pltpu.VMEM((FFN_TM, D), f32)],
        compiler_params=cparams(dimension_semantics=("arbitrary",)),
        name="ffn",
    )(h1.reshape(B * S, D), wf[:, :D_FF], wf[:, D_FF:], w_ffn_out[0].astype(bf16),
      row(ln2_g[0]), row(ln2_b[0]))
    return out.reshape(B, S, D)
```

```python
import functools

import numpy as np
import jax
import jax.numpy as jnp
from jax import lax
from jax.experimental import pallas as pl
from jax.experimental.pallas import tpu as pltpu

D_MODEL = 1024
N_META = 16
BLOCK = 128
HG_HEADS = 4
HG_K = 128
HG_W = HG_HEADS * HG_K
ATT_HEADS = 8
ATT_KV_HEADS = 2
HEAD_DIM = 64
ATT_QW = ATT_HEADS * HEAD_DIM
ATT_KVW = ATT_KV_HEADS * HEAD_DIM
D_FF = 2816
EPS = 1e-5
DEPTH = 1
ALPHA = (2.0 * DEPTH) ** 0.25
ROPE_THETA = 10000.0
NEG = -1e30
LOG2E = 1.4426950408889634

N_LEVELS = 7
DIAG_ID = N_LEVELS
N_MXU_LEVELS = 2
BB = 4
FFN_TM = 512
MXU_N = 256
SUBLANES = 8
FFN_SPLITS = (0, 6 * MXU_N, D_FF)
DENSE_TN = 512
VMEM_LIMIT = 56 * 1024 * 1024


def _level_tables():
    n = BLOCK
    t = np.arange(n)[:, None]
    j = np.arange(n)[None, :]
    blocks = [(j <= t)]
    lv = np.full((n, n), -1, np.int32)
    s = np.arange(n)[None, :]
    for l in range(N_LEVELS):
        c = n >> (l + 1)
        mid = (t // (2 * c)) * (2 * c) + c - 1
        second = (t % (2 * c)) >= c
        if l >= N_LEVELS - N_MXU_LEVELS:
            blocks.append(np.where(second, (j > mid) & (j <= t), (j > t) & (j <= mid)))
        same = (t // (2 * c)) == (s // (2 * c))
        lv = np.where(same & second & ((s % (2 * c)) < c), l, lv)
    lv = np.where(t == s, DIAG_ID, lv).astype(np.int32)
    w = np.concatenate(blocks, axis=0).astype(np.float32)
    return np.concatenate([w, w], axis=1), lv


def _attn_bias():
    r = np.arange(BLOCK)[:, None]
    c = np.arange(2 * BLOCK)[None, :]
    cur = (c >= BLOCK) & ((c - BLOCK) <= r)
    prev = (c < BLOCK) & (c > r)
    b0 = np.where(cur, 0.0, NEG)
    b1 = np.where(cur | prev, 0.0, NEG)
    return np.stack([b0, b1]).astype(np.float32)


def _sigmoid(x):
    return 1.0 / (1.0 + jnp.exp(-x))


def _layer_norm(x, g, b):
    mu = jnp.mean(x, axis=-1, keepdims=True)
    xc = x - mu
    var = jnp.mean(xc * xc, axis=-1, keepdims=True)
    return xc * lax.rsqrt(var + EPS) * g + b


def _dot(a, b):
    return jnp.dot(a, b, preferred_element_type=jnp.float32)


def _dot_nt(a, b):
    return lax.dot_general(a, b, (((1,), (1,)), ((), ())), preferred_element_type=jnp.float32)


def _dot_tn(a, b):
    return lax.dot_general(a, b, (((0,), (0,)), ((), ())), preferred_element_type=jnp.float32)


def _split2(x):
    hi = x.astype(jnp.bfloat16)
    lo = (x - hi.astype(jnp.float32)).astype(jnp.bfloat16)
    return jnp.concatenate([hi, lo], axis=0)


def _rope(x, cos, sin):
    w = x.shape[1]
    lane = lax.broadcasted_iota(jnp.int32, x.shape, 1)
    partner = jnp.where((lane & (HEAD_DIM - 1)) < HEAD_DIM // 2,
                        pltpu.roll(x, w - HEAD_DIM // 2, 1), pltpu.roll(x, HEAD_DIM // 2, 1))
    reps = w // cos.shape[1]
    if reps > 1:
        cos = jnp.concatenate([cos] * reps, axis=1)
        sin = jnp.concatenate([sin] * reps, axis=1)
    return x * cos + partner * sin


def _lo_hi(a, group):
    lane = lax.broadcasted_iota(jnp.int32, a.shape, 1)
    swapped = pltpu.roll(a, HEAD_DIM, 1)
    zero = jnp.zeros_like(a)
    if group == 0:
        lo = jnp.where(lane < HEAD_DIM, a, zero)
        hi = jnp.where(lane >= HEAD_DIM, swapped, zero)
    else:
        lo = jnp.where(lane < HEAD_DIM, swapped, zero)
        hi = jnp.where(lane >= HEAD_DIM, a, zero)
    return jnp.concatenate([lo, hi], axis=0)


def _meta_kernel(meta_ref, g_ref, b_ref, wf_ref, wi_ref, wk_ref, wv_ref, lb_ref, cos_ref, sin_ref,
                 ucat_ref, state_ref, kkm_ref, vvm_ref):
    h = _layer_norm(meta_ref[...], g_ref[...], b_ref[...])
    hb = h.astype(jnp.bfloat16)
    lb = lb_ref[...]
    fg = lb + (1.0 - lb) * _sigmoid(_dot(hb, wf_ref[...]))
    log2_f = jnp.log2(fg)
    kk = 1.0 - fg
    vb = _dot(hb, wi_ref[...]).astype(jnp.bfloat16)
    suffix = _dot(ucat_ref[...], _split2(log2_f))
    ks = (kk * jnp.exp2(suffix)).astype(jnp.bfloat16)
    for hh in range(HG_HEADS):
        cols = slice(hh * HG_K, (hh + 1) * HG_K)
        state_ref[hh] = _dot_tn(vb[:, cols], ks[:, cols])
    k_rot = _rope(_dot(hb, wk_ref[...]), cos_ref[...], sin_ref[...])
    v = _dot(hb, wv_ref[...])
    pad = jnp.zeros((HEAD_DIM - N_META, 2 * HEAD_DIM), jnp.float32)
    for g in range(ATT_KV_HEADS):
        for src, dst in ((k_rot, kkm_ref), (v, vvm_ref)):
            lh = _lo_hi(src, g)
            dst[g] = jnp.concatenate([lh[:N_META], pad, lh[N_META:], pad], axis=0).astype(jnp.bfloat16)


def _hgrn2_pre(rows, phg_scr, kk_scr, wcat_ref):
    q = phg_scr[rows, 0 * HG_W:1 * HG_W]
    log2_f = phg_scr[rows, 1 * HG_W:2 * HG_W]
    pi = phg_scr[rows, 2 * HG_W:3 * HG_W]
    gate = phg_scr[rows, 3 * HG_W:4 * HG_W]
    kk = kk_scr[rows, :]
    e_mxu = _dot(wcat_ref[...], _split2(log2_f))
    b_cum = e_mxu[0:BLOCK]
    b_last = b_cum[BLOCK - 1:BLOCK, :]
    return dict(
        q=q, kk=kk, gate=gate, vb=pi.astype(jnp.bfloat16), b_cum=b_cum, e_mxu=e_mxu,
        q_in=(q * jnp.exp2(b_cum)).astype(jnp.bfloat16),
        k_st=(kk * jnp.exp2(b_last - b_cum)).astype(jnp.bfloat16),
        decay_last=jnp.exp2(b_last))


def _level_exponent(pre, l, cols):
    c = BLOCK >> (l + 1)
    if l >= N_LEVELS - N_MXU_LEVELS:
        i = 1 + l - (N_LEVELS - N_MXU_LEVELS)
        return pre["e_mxu"][i * BLOCK:(i + 1) * BLOCK, cols]
    b3 = pre["b_cum"][:, cols].reshape(BLOCK // (2 * c), 2 * c, HG_K)
    b_mid = b3[:, c - 1:c, :]
    if c >= SUBLANES:
        e3 = jnp.concatenate([b_mid - b3[:, :c], b3[:, c:] - b_mid], axis=1)
    else:
        e3 = -jnp.abs(b3 - b_mid)
    return e3.reshape(BLOCK, HG_K)


def _hgrn2_levels(pre, hh):
    cols = slice(hh * HG_K, (hh + 1) * HG_K)
    qh = pre["q"][:, cols]
    kh = pre["kk"][:, cols]
    ps = []
    for l in range(N_LEVELS):
        x = jnp.exp2(_level_exponent(pre, l, cols))
        c = BLOCK >> (l + 1)
        if c >= SUBLANES:
            q3 = qh.reshape(BLOCK // (2 * c), 2 * c, HG_K)
            k3 = kh.reshape(BLOCK // (2 * c), 2 * c, HG_K)
            z = jnp.concatenate([k3[:, :c], q3[:, c:]], axis=1).reshape(BLOCK, HG_K)
            zb = (z * x).astype(jnp.bfloat16)
            ps.append(_dot_nt(zb, zb))
        else:
            ps.append(_dot_nt((qh * x).astype(jnp.bfloat16), (kh * x).astype(jnp.bfloat16)))
    ps.append(_dot_nt(qh.astype(jnp.bfloat16), kh.astype(jnp.bfloat16)))
    return ps


def _hgrn2_finish(pre, hh, ps, b, rows, state_scr, ypre_scr, lv_ref, normg_ref):
    cols = slice(hh * HG_K, (hh + 1) * HG_K)
    lv = lv_ref[...]
    vb = pre["vb"][:, cols]
    a = jnp.zeros((BLOCK, BLOCK), jnp.float32)
    for l, p in enumerate(ps):
        a = jnp.where(lv == l, p, a)
    st = state_scr[b, hh]
    o = _dot_nt(pre["q_in"][:, cols], st.astype(jnp.bfloat16)) + _dot(a.astype(jnp.bfloat16), vb)
    state_scr[b, hh] = st * pre["decay_last"][:, cols] + _dot_tn(vb, pre["k_st"][:, cols])
    r = lax.rsqrt(jnp.mean(o * o, axis=-1, keepdims=True) + EPS)
    ypre_scr[rows, cols] = (o * r * normg_ref[:, cols] * pre["gate"][:, cols]).astype(jnp.bfloat16)


def _attn_pre(b, rows, chunk, patt_scr, kband_scr, vband_scr, bias_ref):
    q = patt_scr[rows, 0:ATT_QW]
    k = patt_scr[rows, ATT_QW:ATT_QW + ATT_KVW]
    v = patt_scr[rows, ATT_QW + ATT_KVW:ATT_QW + 2 * ATT_KVW]
    kband = jnp.concatenate([kband_scr[b], k], axis=0)
    vband = jnp.concatenate([vband_scr[b], v], axis=0)
    kband_scr[b] = k
    vband_scr[b] = v
    return dict(qb=q.astype(jnp.bfloat16), kband=kband, vband=vband,
                bias=bias_ref[jnp.minimum(chunk, 1)])


def _attn_scores(pre, g, kkm_ref):
    qb = pre["qb"]
    qs = jnp.concatenate([qb[:, (2 * g) * 128:(2 * g + 1) * 128],
                          qb[:, (2 * g + 1) * 128:(2 * g + 2) * 128]], axis=0)
    s_band = _dot_nt(qs, _lo_hi(pre["kband"], g).astype(jnp.bfloat16))
    s_meta = _dot_nt(qs, kkm_ref[g])
    return s_band, s_meta


def _attn_softmax(pre, g, scores, sinks_ref):
    s_band, s_meta = scores
    bias = pre["bias"]
    lane = lax.broadcasted_iota(jnp.int32, (BLOCK, 2 * HEAD_DIM), 1)
    meta_mask = (lane < N_META, (lane >= HEAD_DIM) & (lane < HEAD_DIM + N_META))
    p_rows, pm_rows, rinv_rows = [], [], []
    for i in range(2):
        e_b, e_m, dinv = [], [], []
        for j in range(2):
            sink = sinks_ref[4 * g + 2 * i + j] * LOG2E
            sb = s_band[i * BLOCK:(i + 1) * BLOCK, j * 2 * BLOCK:(j + 1) * 2 * BLOCK] + bias
            sm = jnp.where(meta_mask[j], s_meta[i * BLOCK:(i + 1) * BLOCK, :], NEG)
            m = jnp.max(jnp.maximum(jnp.maximum(sb[:, :BLOCK], sb[:, BLOCK:]), sm), axis=-1, keepdims=True)
            m = jnp.maximum(m, sink)
            eb = jnp.exp2(sb - m)
            em = jnp.exp2(sm - m)
            d = (jnp.sum(eb[:, :BLOCK] + eb[:, BLOCK:] + em, axis=-1, keepdims=True)
                 + jnp.exp2(sink - m))
            e_b.append(eb.astype(jnp.bfloat16))
            e_m.append(em)
            dinv.append(1.0 / d)
        p_rows.append(jnp.concatenate(e_b, axis=1))
        pm_rows.append((e_m[0] + e_m[1]).astype(jnp.bfloat16))
        rinv_rows.append(jnp.where(lane < HEAD_DIM, dinv[0], dinv[1]))
    return (jnp.concatenate(p_rows, axis=0), jnp.concatenate(pm_rows, axis=0),
            jnp.concatenate(rinv_rows, axis=0))


def _attn_out(pre, g, probs, rows, opre_scr, vvm_ref):
    p, pm, rinv = probs
    o = _dot(p, _lo_hi(pre["vband"], g).astype(jnp.bfloat16)) + _dot(pm, vvm_ref[g])
    o = o * rinv
    opre_scr[rows, (2 * g) * 128:(2 * g + 1) * 128] = o[0:BLOCK].astype(jnp.bfloat16)
    opre_scr[rows, (2 * g + 1) * 128:(2 * g + 2) * 128] = o[BLOCK:2 * BLOCK].astype(jnp.bfloat16)


def _mixer_kernel(x_ref, x0_ref, cos_ref, sin_ref, cos0_ref, sin0_ref, bias_ref, lv_ref, wcat_ref, lng_ref, lnb_ref,
                  whg_ref, watt_ref, wgate_ref, lb_ref, normg_ref, sinks_ref,
                  wbh_ref, wba_ref, wout_ref, ln1g_ref, ln1b_ref, state0_ref, kkm_ref, vvm_ref,
                  out_ref,
                  state_scr, kband_scr, vband_scr, h_scr, hb_scr, phg_scr, patt_scr, ypre_scr, opre_scr,
                  mixin_scr, res_scr, kk_scr):
    chunk = pl.program_id(1)
    m = BB * BLOCK

    def embed(x_blocks):
        return _layer_norm(x_blocks.reshape(m, D_MODEL), lng_ref[...], lnb_ref[...])

    n_hg_blocks = 4 * HG_W // MXU_N
    n_proj = n_hg_blocks + (ATT_QW + 2 * ATT_KVW) // MXU_N

    def rope_rows(x, cos, sin):
        return jnp.concatenate([_rope(x[b * BLOCK:(b + 1) * BLOCK], cos, sin) for b in range(BB)], axis=0)

    def input_proj(hb, blocks, cos_r, sin_r):
        blocks_per_section = HG_W // MXU_N
        for k in blocks:
            if k < n_hg_blocks:
                cols = slice(k * MXU_N, (k + 1) * MXU_N)
                p = _dot(hb, whg_ref[:, cols])
                section = k // blocks_per_section
                if section == 1:
                    sub = slice(cols.start - HG_W, cols.stop - HG_W)
                    lb = lb_ref[:, sub]
                    fg = lb + (1.0 - lb) * _sigmoid(p)
                    kk_scr[:, sub] = 1.0 - fg
                    p = jnp.log2(fg)
                elif section in (0, 3):
                    p = p * _sigmoid(p)
                phg_scr[:, cols] = p
            else:
                cols = slice((k - n_hg_blocks) * MXU_N, (k - n_hg_blocks + 1) * MXU_N)
                p = _dot(hb, watt_ref[:, cols])
                cos, sin = cos_r[...], sin_r[...]
                if cols.stop <= ATT_QW:
                    p = rope_rows(p, cos, sin) * (HEAD_DIM ** -0.5 * LOG2E)
                else:
                    p = jnp.concatenate([rope_rows(p[:, :ATT_KVW], cos, sin), p[:, ATT_KVW:]], axis=1)
                patt_scr[:, cols] = p

    @pl.when(chunk == 0)
    def _():
        for b in range(BB):
            state_scr[b] = state0_ref[...]
        kband_scr[...] = jnp.zeros_like(kband_scr)
        vband_scr[...] = jnp.zeros_like(vband_scr)
        h0 = embed(x0_ref[...])
        h_scr[...] = h0
        hb_scr[...] = h0.astype(jnp.bfloat16)
        input_proj(hb_scr[...], range(n_proj), cos0_ref, sin0_ref)

    heads, groups = [], []
    for b in range(BB):
        rows = slice(b * BLOCK, (b + 1) * BLOCK)
        hg = _hgrn2_pre(rows, phg_scr, kk_scr, wcat_ref)
        at = _attn_pre(b, rows, chunk, patt_scr, kband_scr, vband_scr, bias_ref)
        heads.append([(hg, hh, b, rows) for hh in range(HG_HEADS)])
        groups.append([(at, g, rows) for g in range(ATT_KV_HEADS)])
    heads = [u for per_head in zip(*heads) for u in per_head]
    groups = [u for per_kv in zip(*groups) for u in per_kv]
    levels, scores, probs = {}, {}, {}

    def head_stage1(n):
        hg, hh, _, _ = heads[n]
        levels[n] = _hgrn2_levels(hg, hh)

    def head_stage2(n):
        hg, hh, b, rows = heads[n]
        _hgrn2_finish(hg, hh, levels.pop(n), b, rows, state_scr, ypre_scr, lv_ref, normg_ref)

    def group_stage1(n):
        at, g, _ = groups[n]
        scores[n] = _attn_scores(at, g, kkm_ref)

    def group_stage2(n):
        at, g, _ = groups[n]
        probs[n] = _attn_softmax(at, g, scores.pop(n), sinks_ref)

    def group_stage3(n):
        at, g, rows = groups[n]
        _attn_out(at, g, probs.pop(n), rows, opre_scr, vvm_ref)

    n_groups = len(groups)
    for n in range(n_groups):
        group_stage1(n)
        head_stage1(2 * n)
        if n > 0:
            head_stage2(2 * n - 1)
        group_stage2(n)
        head_stage1(2 * n + 1)
        head_stage2(2 * n)
        group_stage3(n)
    head_stage2(2 * n_groups - 1)

    h_next = embed(x_ref[...])
    hb_next = h_next.astype(jnp.bfloat16)
    for k in range(D_MODEL // DENSE_TN):
        cols = slice(k * DENSE_TN, (k + 1) * DENSE_TN)
        cols2 = slice(D_MODEL + k * DENSE_TN, D_MODEL + (k + 1) * DENSE_TN)
        g_hg = _sigmoid(_dot(hb_scr[...], wgate_ref[:, cols]))
        g_att = _sigmoid(_dot(hb_scr[...], wgate_ref[:, cols2]))
        y_hg = _dot(ypre_scr[...], wbh_ref[:, cols])
        y_att = _dot(opre_scr[...], wba_ref[:, cols])
        mixin_scr[:, cols] = (g_hg * y_hg + g_att * y_att).astype(jnp.bfloat16)
    for k in range(D_MODEL // DENSE_TN):
        cols = slice(k * DENSE_TN, (k + 1) * DENSE_TN)
        res_scr[:, cols] = ALPHA * h_scr[:, cols] + _dot(mixin_scr[...], wout_ref[:, cols])
    h_scr[...] = h_next
    hb_scr[...] = hb_next
    input_proj(hb_next, range(0, 4), cos_ref, sin_ref)
    out = _layer_norm(res_scr[...], ln1g_ref[...], ln1b_ref[...])
    out_ref[...] = out.reshape(BB, BLOCK, D_MODEL)
    input_proj(hb_next, range(4, n_proj), cos_ref, sin_ref)


def _ffn_kernel(h_ref, wa_ref, wu_ref, wo_ref, g_ref, b_ref, out_ref, res_scr):
    @pl.when(pl.program_id(0) == 0)
    def _():
        res_scr[...] = jnp.zeros_like(res_scr)

    h = h_ref[...]
    hb = h.astype(jnp.bfloat16)
    acc = jnp.zeros((FFN_TM, D_MODEL), jnp.float32)
    for n, (lo, hi) in enumerate(zip(FFN_SPLITS[:-1], FFN_SPLITS[1:])):
        sl = slice(lo, hi)
        a = _dot(hb, wa_ref[:, sl])
        u = _dot(hb, wu_ref[:, sl])
        hid = (a * _sigmoid(a) * u).astype(jnp.bfloat16)
        acc = acc + _dot(hid, wo_ref[sl, :])
        if n == 0:
            out_ref[...] = _layer_norm(res_scr[...], g_ref[...], b_ref[...])
    res_scr[...] = ALPHA * h + acc


def _const_spec(shape):
    nd = len(shape)
    return pl.BlockSpec(shape, lambda *_: (0,) * nd, pipeline_mode=pl.Buffered(1))


def kernel(x, meta_tokens, ln_emb_g, ln_emb_b, w_in, hg_lower_bounds, hg_norm_g, attn_sinks,
           w_branch_hg, w_branch_attn, w_out, ln1_g, ln1_b, w_ffn_in, w_ffn_out, ln2_g, ln2_b):
    B, S, D = x.shape
    assert D == D_MODEL and S % BLOCK == 0 and B % BB == 0 and (B * S) % FFN_TM == 0
    n_chunks = S // BLOCK
    f32, bf16 = jnp.float32, jnp.bfloat16
    row = lambda v: v.astype(f32).reshape(1, -1)

    lb = jnp.cumsum(jax.nn.softmax(hg_lower_bounds.astype(f32), axis=0), axis=0)[0].reshape(1, HG_W)
    normg = jnp.tile(hg_norm_g[0].astype(f32), HG_HEADS).reshape(1, HG_W)
    w0 = w_in[0].astype(bf16)
    o_att = 4 * HG_W
    o_gate = o_att + ATT_QW + 2 * ATT_KVW
    w_hg, w_att, w_gate = w0[:, :o_att], w0[:, o_att:o_gate], w0[:, o_gate:]
    half = HEAD_DIM // 2
    inv = ROPE_THETA ** (-jnp.arange(half, dtype=f32) / half)
    ang = jnp.arange(S + N_META, dtype=jnp.int32).astype(f32)[:, None] * inv[None, :]
    cos_t = jnp.tile(jnp.cos(ang), (1, 4))
    sin_t = jnp.tile(jnp.concatenate([-jnp.sin(ang), jnp.sin(ang)], axis=1), (1, 2))
    wcat_np, lv_np = _level_tables()
    wcat = jnp.asarray(wcat_np, bf16)
    lv = jnp.asarray(lv_np)
    ucat = jnp.asarray(np.tile(np.triu(np.ones((N_META, N_META), np.float32), 1), (1, 2)), bf16)
    bias = jnp.asarray(_attn_bias())
    cparams = functools.partial(pltpu.CompilerParams, vmem_limit_bytes=VMEM_LIMIT)

    state0, kkm, vvm = pl.pallas_call(
        _meta_kernel,
        out_shape=(jax.ShapeDtypeStruct((HG_HEADS, HG_K, HG_K), f32),
                   jax.ShapeDtypeStruct((ATT_KV_HEADS, BLOCK, 2 * HEAD_DIM), bf16),
                   jax.ShapeDtypeStruct((ATT_KV_HEADS, BLOCK, 2 * HEAD_DIM), bf16)),
        name="meta",
    )(meta_tokens.astype(f32), row(ln_emb_g), row(ln_emb_b),
      w_hg[:, HG_W:2 * HG_W], w_hg[:, 2 * HG_W:3 * HG_W],
      w_att[:, ATT_QW:ATT_QW + ATT_KVW], w_att[:, ATT_QW + ATT_KVW:],
      lb, cos_t[:N_META], sin_t[:N_META], ucat)

    m = BB * BLOCK
    h1 = pl.pallas_call(
        _mixer_kernel,
        grid=(B // BB, n_chunks),
        in_specs=[
            pl.BlockSpec((BB, BLOCK, D), lambda i, c: (i, jnp.minimum(c + 1, n_chunks - 1), 0)),
            pl.BlockSpec((BB, BLOCK, D), lambda i, c: (i, 0, 0)),
            pl.BlockSpec((BLOCK, 2 * HEAD_DIM), lambda i, c: (jnp.minimum(c + 1, n_chunks - 1), 0)),
            pl.BlockSpec((BLOCK, 2 * HEAD_DIM), lambda i, c: (jnp.minimum(c + 1, n_chunks - 1), 0)),
            pl.BlockSpec((BLOCK, 2 * HEAD_DIM), lambda i, c: (0, 0)),
            pl.BlockSpec((BLOCK, 2 * HEAD_DIM), lambda i, c: (0, 0)),
            _const_spec(bias.shape), _const_spec(lv.shape), _const_spec(wcat.shape),
            _const_spec((1, D)), _const_spec((1, D)),
            _const_spec(w_hg.shape), _const_spec(w_att.shape), _const_spec(w_gate.shape),
            _const_spec((1, HG_W)), _const_spec((1, HG_W)),
            pl.BlockSpec(memory_space=pltpu.SMEM),
            _const_spec((HG_W, D)), _const_spec((ATT_QW, D)), _const_spec((D, D)),
            _const_spec((1, D)), _const_spec((1, D)),
            _const_spec(state0.shape), _const_spec(kkm.shape), _const_spec(vvm.shape),
        ],
        out_specs=pl.BlockSpec((BB, BLOCK, D), lambda i, c: (i, c, 0)),
        out_shape=jax.ShapeDtypeStruct((B, S, D), f32),
        scratch_shapes=[
            pltpu.VMEM((BB, HG_HEADS, HG_K, HG_K), f32),
            pltpu.VMEM((BB, BLOCK, ATT_KVW), f32),
            pltpu.VMEM((BB, BLOCK, ATT_KVW), f32),
            pltpu.VMEM((m, D), f32),
            pltpu.VMEM((m, D), bf16),
            pltpu.VMEM((m, 4 * HG_W), f32),
            pltpu.VMEM((m, ATT_QW + 2 * ATT_KVW), f32),
            pltpu.VMEM((m, HG_W), bf16),
            pltpu.VMEM((m, ATT_QW), bf16),
            pltpu.VMEM((m, D), bf16),
            pltpu.VMEM((m, D), f32),
            pltpu.VMEM((m, HG_W), f32),
        ],
        compiler_params=cparams(dimension_semantics=("arbitrary", "arbitrary")),
        name="mixer",
    )(x.astype(f32), x.astype(f32), cos_t[N_META:], sin_t[N_META:], cos_t[N_META:], sin_t[N_META:],
      bias, lv, wcat,
      row(ln_emb_g), row(ln_emb_b),
      w_hg, w_att, w_gate, lb, normg, attn_sinks[0].astype(f32),
      w_branch_hg[0].astype(bf16), w_branch_attn[0].astype(bf16), w_out[0].astype(bf16),
      row(ln1_g[0]), row(ln1_b[0]), state0, kkm, vvm)

    wf = w_ffn_in[0].astype(bf16)
    n_tiles = B * S // FFN_TM
    out = pl.pallas_call(
        _ffn_kernel,
        grid=(n_tiles + 1,),
        in_specs=[
            pl.BlockSpec((FFN_TM, D), lambda i: (jnp.minimum(i, n_tiles - 1), 0)),
            _const_spec((D, D_FF)), _const_spec((D, D_FF)), _const_spec((D_FF, D)),
            _const_spec((1, D)), _const_spec((1, D)),
        ],
        out_specs=pl.BlockSpec((FFN_TM, D), lambda i: (jnp.maximum(i - 1, 0), 0)),
        out_shape=jax.ShapeDtypeStruct((B * S, D), f32),
        scratch_shapes=[pltpu.VMEM((FFN_TM, D), f32)],
        compiler_params=cparams(dimension_semantics=("arbitrary",)),
        name="ffn",
    )(h1.reshape(B * S, D), wf[:, :D_FF], wf[:, D_FF:], w_ffn_out[0].astype(bf16),
      row(ln2_g[0]), row(ln2_b[0]))
    return out.reshape(B, S, D)
```

```python
import functools

import numpy as np
import jax
import jax.numpy as jnp
from jax import lax
from jax.experimental import pallas as pl
from jax.experimental.pallas import tpu as pltpu

D_MODEL = 1024
N_META = 16
BLOCK = 128
HG_HEADS = 4
HG_K = 128
HG_W = HG_HEADS * HG_K
ATT_HEADS = 8
ATT_KV_HEADS = 2
HEAD_DIM = 64
ATT_QW = ATT_HEADS * HEAD_DIM
ATT_KVW = ATT_KV_HEADS * HEAD_DIM
D_FF = 2816
EPS = 1e-5
DEPTH = 1
ALPHA = (2.0 * DEPTH) ** 0.25
ROPE_THETA = 10000.0
NEG = -1e30
LOG2E = 1.4426950408889634

N_LEVELS = 7
DIAG_ID = N_LEVELS
MXU_LEVEL = N_LEVELS - 2
BB = 4
FFN_TM = 512
MXU_N = 256
SUBLANES = 8
FFN_SPLITS = (0, 6 * MXU_N, D_FF)
DENSE_TN = 512
VMEM_LIMIT = 56 * 1024 * 1024


def _level_tables():
    n = BLOCK
    t = np.arange(n)[:, None]
    j = np.arange(n)[None, :]
    blocks = [(j <= t)]
    lv = np.full((n, n), -1, np.int32)
    s = np.arange(n)[None, :]
    for l in range(N_LEVELS):
        c = n >> (l + 1)
        mid = (t // (2 * c)) * (2 * c) + c - 1
        second = (t % (2 * c)) >= c
        if l == MXU_LEVEL:
            blocks.append(np.where(second, (j > mid) & (j <= t), (j > t) & (j <= mid)))
        same = (t // (2 * c)) == (s // (2 * c))
        lv = np.where(same & second & ((s % (2 * c)) < c), l, lv)
    lv = np.where(t == s, DIAG_ID, lv).astype(np.int32)
    w = np.concatenate(blocks, axis=0).astype(np.float32)
    return np.concatenate([w, w], axis=1), lv


def _attn_bias():
    r = np.arange(BLOCK)[:, None]
    c = np.arange(2 * BLOCK)[None, :]
    cur = (c >= BLOCK) & ((c - BLOCK) <= r)
    prev = (c < BLOCK) & (c > r)
    b0 = np.where(cur, 0.0, NEG)
    b1 = np.where(cur | prev, 0.0, NEG)
    return np.stack([b0, b1]).astype(np.float32)


def _sigmoid(x):
    return 1.0 / (1.0 + jnp.exp(-x))


def _layer_norm(x, g, b):
    mu = jnp.mean(x, axis=-1, keepdims=True)
    xc = x - mu
    var = jnp.mean(xc * xc, axis=-1, keepdims=True)
    return xc * lax.rsqrt(var + EPS) * g + b


def _dot(a, b):
    return jnp.dot(a, b, preferred_element_type=jnp.float32)


def _dot_nt(a, b):
    return lax.dot_general(a, b, (((1,), (1,)), ((), ())), preferred_element_type=jnp.float32)


def _dot_tn(a, b):
    return lax.dot_general(a, b, (((0,), (0,)), ((), ())), preferred_element_type=jnp.float32)


def _split2(x):
    hi = x.astype(jnp.bfloat16)
    lo = (x - hi.astype(jnp.float32)).astype(jnp.bfloat16)
    return jnp.concatenate([hi, lo], axis=0)


def _rope(x, cos, sin):
    w = x.shape[1]
    lane = lax.broadcasted_iota(jnp.int32, x.shape, 1)
    partner = jnp.where((lane & (HEAD_DIM - 1)) < HEAD_DIM // 2,
                        pltpu.roll(x, w - HEAD_DIM // 2, 1), pltpu.roll(x, HEAD_DIM // 2, 1))
    reps = w // cos.shape[1]
    if reps > 1:
        cos = jnp.concatenate([cos] * reps, axis=1)
        sin = jnp.concatenate([sin] * reps, axis=1)
    return x * cos + partner * sin


def _lo_hi(a, group):
    lane = lax.broadcasted_iota(jnp.int32, a.shape, 1)
    swapped = pltpu.roll(a, HEAD_DIM, 1)
    zero = jnp.zeros_like(a)
    if group == 0:
        lo = jnp.where(lane < HEAD_DIM, a, zero)
        hi = jnp.where(lane >= HEAD_DIM, swapped, zero)
    else:
        lo = jnp.where(lane < HEAD_DIM, swapped, zero)
        hi = jnp.where(lane >= HEAD_DIM, a, zero)
    return jnp.concatenate([lo, hi], axis=0)


def _meta_kernel(meta_ref, g_ref, b_ref, wf_ref, wi_ref, wk_ref, wv_ref, lb_ref, cos_ref, sin_ref,
                 ucat_ref, state_ref, kkm_ref, vvm_ref):
    h = _layer_norm(meta_ref[...], g_ref[...], b_ref[...])
    hb = h.astype(jnp.bfloat16)
    lb = lb_ref[...]
    fg = lb + (1.0 - lb) * _sigmoid(_dot(hb, wf_ref[...]))
    log2_f = jnp.log2(fg)
    kk = 1.0 - fg
    vb = _dot(hb, wi_ref[...]).astype(jnp.bfloat16)
    suffix = _dot(ucat_ref[...], _split2(log2_f))
    ks = (kk * jnp.exp2(suffix)).astype(jnp.bfloat16)
    for hh in range(HG_HEADS):
        cols = slice(hh * HG_K, (hh + 1) * HG_K)
        state_ref[hh] = _dot_tn(vb[:, cols], ks[:, cols])
    k_rot = _rope(_dot(hb, wk_ref[...]), cos_ref[...], sin_ref[...])
    v = _dot(hb, wv_ref[...])
    pad = jnp.zeros((HEAD_DIM - N_META, 2 * HEAD_DIM), jnp.float32)
    for g in range(ATT_KV_HEADS):
        for src, dst in ((k_rot, kkm_ref), (v, vvm_ref)):
            lh = _lo_hi(src, g)
            dst[g] = jnp.concatenate([lh[:N_META], pad, lh[N_META:], pad], axis=0).astype(jnp.bfloat16)


def _hgrn2_pre(rows, phg_scr, kk_scr, wcat_ref):
    q = phg_scr[rows, 0 * HG_W:1 * HG_W]
    log2_f = phg_scr[rows, 1 * HG_W:2 * HG_W]
    pi = phg_scr[rows, 2 * HG_W:3 * HG_W]
    gate = phg_scr[rows, 3 * HG_W:4 * HG_W]
    kk = kk_scr[rows, :]
    e_mxu = _dot(wcat_ref[...], _split2(log2_f))
    b_cum = e_mxu[0:BLOCK]
    b_last = b_cum[BLOCK - 1:BLOCK, :]
    return dict(
        q=q, kk=kk, gate=gate, vb=pi.astype(jnp.bfloat16), b_cum=b_cum, e_mxu=e_mxu, log2_f=log2_f,
        q_in=(q * jnp.exp2(b_cum)).astype(jnp.bfloat16),
        k_st=(kk * jnp.exp2(b_last - b_cum)).astype(jnp.bfloat16),
        decay_last=jnp.exp2(b_last))


def _level_exponent(pre, l, cols):
    c = BLOCK >> (l + 1)
    if c == 1:
        lf = pre["log2_f"][:, cols]
        row = lax.broadcasted_iota(jnp.int32, lf.shape, 0)
        return jnp.where((row & 1) == 1, lf, 0.0)
    if l == MXU_LEVEL:
        return pre["e_mxu"][BLOCK:2 * BLOCK, cols]
    b3 = pre["b_cum"][:, cols].reshape(BLOCK // (2 * c), 2 * c, HG_K)
    b_mid = b3[:, c - 1:c, :]
    if c >= SUBLANES:
        e3 = jnp.concatenate([b_mid - b3[:, :c], b3[:, c:] - b_mid], axis=1)
    else:
        e3 = -jnp.abs(b3 - b_mid)
    return e3.reshape(BLOCK, HG_K)


def _hgrn2_levels(pre, hh):
    cols = slice(hh * HG_K, (hh + 1) * HG_K)
    qh = pre["q"][:, cols]
    kh = pre["kk"][:, cols]
    ps = []
    for l in range(N_LEVELS):
        x = jnp.exp2(_level_exponent(pre, l, cols))
        c = BLOCK >> (l + 1)
        if c >= SUBLANES:
            q3 = qh.reshape(BLOCK // (2 * c), 2 * c, HG_K)
            k3 = kh.reshape(BLOCK // (2 * c), 2 * c, HG_K)
            z = jnp.concatenate([k3[:, :c], q3[:, c:]], axis=1).reshape(BLOCK, HG_K)
            zb = (z * x).astype(jnp.bfloat16)
            ps.append(_dot_nt(zb, zb))
        else:
            ps.append(_dot_nt((qh * x).astype(jnp.bfloat16), (kh * x).astype(jnp.bfloat16)))
    ps.append(_dot_nt(qh.astype(jnp.bfloat16), kh.astype(jnp.bfloat16)))
    return ps


def _hgrn2_finish(pre, hh, ps, b, rows, state_scr, ypre_scr, lv_ref, normg_ref):
    cols = slice(hh * HG_K, (hh + 1) * HG_K)
    lv = lv_ref[...]
    vb = pre["vb"][:, cols]
    a = jnp.zeros((BLOCK, BLOCK), jnp.float32)
    for l, p in enumerate(ps):
        a = jnp.where(lv == l, p, a)
    st = state_scr[b, hh]
    o = _dot_nt(pre["q_in"][:, cols], st.astype(jnp.bfloat16)) + _dot(a.astype(jnp.bfloat16), vb)
    state_scr[b, hh] = st * pre["decay_last"][:, cols] + _dot_tn(vb, pre["k_st"][:, cols])
    r = lax.rsqrt(jnp.mean(o * o, axis=-1, keepdims=True) + EPS)
    ypre_scr[rows, cols] = (o * r * normg_ref[:, cols] * pre["gate"][:, cols]).astype(jnp.bfloat16)


def _attn_pre(b, rows, chunk, patt_scr, kband_scr, vband_scr, bias_ref):
    q = patt_scr[rows, 0:ATT_QW]
    k = patt_scr[rows, ATT_QW:ATT_QW + ATT_KVW]
    v = patt_scr[rows, ATT_QW + ATT_KVW:ATT_QW + 2 * ATT_KVW]
    kband = jnp.concatenate([kband_scr[b], k], axis=0)
    vband = jnp.concatenate([vband_scr[b], v], axis=0)
    kband_scr[b] = k
    vband_scr[b] = v
    return dict(qb=q.astype(jnp.bfloat16), kband=kband, vband=vband,
                bias=bias_ref[jnp.minimum(chunk, 1)])


def _attn_scores(pre, g, kkm_ref):
    qb = pre["qb"]
    qs = jnp.concatenate([qb[:, (2 * g) * 128:(2 * g + 1) * 128],
                          qb[:, (2 * g + 1) * 128:(2 * g + 2) * 128]], axis=0)
    s_band = _dot_nt(qs, _lo_hi(pre["kband"], g).astype(jnp.bfloat16))
    s_meta = _dot_nt(qs, kkm_ref[g])
    return s_band, s_meta


def _attn_softmax(pre, g, scores, sinks_ref):
    s_band, s_meta = scores
    bias = pre["bias"]
    lane = lax.broadcasted_iota(jnp.int32, (BLOCK, 2 * HEAD_DIM), 1)
    meta_mask = (lane < N_META, (lane >= HEAD_DIM) & (lane < HEAD_DIM + N_META))
    p_rows, pm_rows, rinv_rows = [], [], []
    for i in range(2):
        e_b, e_m, dinv = [], [], []
        for j in range(2):
            sink = sinks_ref[4 * g + 2 * i + j] * LOG2E
            sb = s_band[i * BLOCK:(i + 1) * BLOCK, j * 2 * BLOCK:(j + 1) * 2 * BLOCK] + bias
            sm = jnp.where(meta_mask[j], s_meta[i * BLOCK:(i + 1) * BLOCK, :], NEG)
            m = jnp.max(jnp.maximum(jnp.maximum(sb[:, :BLOCK], sb[:, BLOCK:]), sm), axis=-1, keepdims=True)
            m = jnp.maximum(m, sink)
            eb = jnp.exp2(sb - m)
            em = jnp.exp2(sm - m)
            d = (jnp.sum(eb[:, :BLOCK] + eb[:, BLOCK:] + em, axis=-1, keepdims=True)
                 + jnp.exp2(sink - m))
            e_b.append(eb.astype(jnp.bfloat16))
            e_m.append(em)
            dinv.append(1.0 / d)
        p_rows.append(jnp.concatenate(e_b, axis=1))
        pm_rows.append((e_m[0] + e_m[1]).astype(jnp.bfloat16))
        rinv_rows.append(jnp.where(lane < HEAD_DIM, dinv[0], dinv[1]))
    return (jnp.concatenate(p_rows, axis=0), jnp.concatenate(pm_rows, axis=0),
            jnp.concatenate(rinv_rows, axis=0))


def _attn_out(pre, g, probs, rows, opre_scr, vvm_ref):
    p, pm, rinv = probs
    o = _dot(p, _lo_hi(pre["vband"], g).astype(jnp.bfloat16)) + _dot(pm, vvm_ref[g])
    o = o * rinv
    opre_scr[rows, (2 * g) * 128:(2 * g + 1) * 128] = o[0:BLOCK].astype(jnp.bfloat16)
    opre_scr[rows, (2 * g + 1) * 128:(2 * g + 2) * 128] = o[BLOCK:2 * BLOCK].astype(jnp.bfloat16)


def _mixer_kernel(x_ref, x0_ref, cos_ref, sin_ref, cos0_ref, sin0_ref, bias_ref, lv_ref, wcat_ref, lng_ref, lnb_ref,
                  whg_ref, watt_ref, wgate_ref, lb_ref, normg_ref, sinks_ref,
                  wbh_ref, wba_ref, wout_ref, ln1g_ref, ln1b_ref, state0_ref, kkm_ref, vvm_ref,
                  out_ref,
                  state_scr, kband_scr, vband_scr, h_scr, hb_scr, phg_scr, patt_scr, ypre_scr, opre_scr,
                  mixin_scr, res_scr, kk_scr):
    chunk = pl.program_id(1)
    m = BB * BLOCK

    def embed(x_blocks):
        return _layer_norm(x_blocks.reshape(m, D_MODEL), lng_ref[...], lnb_ref[...])

    n_hg_blocks = 4 * HG_W // MXU_N
    n_proj = n_hg_blocks + (ATT_QW + 2 * ATT_KVW) // MXU_N

    def rope_rows(x, cos, sin):
        return jnp.concatenate([_rope(x[b * BLOCK:(b + 1) * BLOCK], cos, sin) for b in range(BB)], axis=0)

    def input_proj(hb, blocks, cos_r, sin_r):
        blocks_per_section = HG_W // MXU_N
        for k in blocks:
            if k < n_hg_blocks:
                cols = slice(k * MXU_N, (k + 1) * MXU_N)
                p = _dot(hb, whg_ref[:, cols])
                section = k // blocks_per_section
                if section == 1:
                    sub = slice(cols.start - HG_W, cols.stop - HG_W)
                    lb = lb_ref[:, sub]
                    fg = lb + (1.0 - lb) * _sigmoid(p)
                    kk_scr[:, sub] = 1.0 - fg
                    p = jnp.log2(fg)
                elif section in (0, 3):
                    p = p * _sigmoid(p)
                phg_scr[:, cols] = p
            else:
                cols = slice((k - n_hg_blocks) * MXU_N, (k - n_hg_blocks + 1) * MXU_N)
                p = _dot(hb, watt_ref[:, cols])
                cos, sin = cos_r[...], sin_r[...]
                if cols.stop <= ATT_QW:
                    p = rope_rows(p, cos, sin) * (HEAD_DIM ** -0.5 * LOG2E)
                else:
                    p = jnp.concatenate([rope_rows(p[:, :ATT_KVW], cos, sin), p[:, ATT_KVW:]], axis=1)
                patt_scr[:, cols] = p

    @pl.when(chunk == 0)
    def _():
        for b in range(BB):
            state_scr[b] = state0_ref[...]
        kband_scr[...] = jnp.zeros_like(kband_scr)
        vband_scr[...] = jnp.zeros_like(vband_scr)
        h0 = embed(x0_ref[...])
        h_scr[...] = h0
        hb_scr[...] = h0.astype(jnp.bfloat16)
        input_proj(hb_scr[...], range(n_proj), cos0_ref, sin0_ref)

    heads, groups = [], []
    for b in range(BB):
        rows = slice(b * BLOCK, (b + 1) * BLOCK)
        hg = _hgrn2_pre(rows, phg_scr, kk_scr, wcat_ref)
        at = _attn_pre(b, rows, chunk, patt_scr, kband_scr, vband_scr, bias_ref)
        heads.append([(hg, hh, b, rows) for hh in range(HG_HEADS)])
        groups.append([(at, g, rows) for g in range(ATT_KV_HEADS)])
    heads = [u for per_head in zip(*heads) for u in per_head]
    groups = [u for per_kv in zip(*groups) for u in per_kv]
    levels, scores, probs = {}, {}, {}

    def head_stage1(n):
        hg, hh, _, _ = heads[n]
        levels[n] = _hgrn2_levels(hg, hh)

    def head_stage2(n):
        hg, hh, b, rows = heads[n]
        _hgrn2_finish(hg, hh, levels.pop(n), b, rows, state_scr, ypre_scr, lv_ref, normg_ref)

    def group_stage1(n):
        at, g, _ = groups[n]
        scores[n] = _attn_scores(at, g, kkm_ref)

    def group_stage2(n):
        at, g, _ = groups[n]
        probs[n] = _attn_softmax(at, g, scores.pop(n), sinks_ref)

    def group_stage3(n):
        at, g, rows = groups[n]
        _attn_out(at, g, probs.pop(n), rows, opre_scr, vvm_ref)

    n_groups = len(groups)
    for n in range(n_groups):
        group_stage1(n)
        head_stage1(2 * n)
        if n > 0:
            head_stage2(2 * n - 1)
        group_stage2(n)
        head_stage1(2 * n + 1)
        head_stage2(2 * n)
        group_stage3(n)
    head_stage2(2 * n_groups - 1)

    h_next = embed(x_ref[...])
    hb_next = h_next.astype(jnp.bfloat16)
    for k in range(D_MODEL // DENSE_TN):
        cols = slice(k * DENSE_TN, (k + 1) * DENSE_TN)
        cols2 = slice(D_MODEL + k * DENSE_TN, D_MODEL + (k + 1) * DENSE_TN)
        g_hg = _sigmoid(_dot(hb_scr[...], wgate_ref[:, cols]))
        g_att = _sigmoid(_dot(hb_scr[...], wgate_ref[:, cols2]))
        y_hg = _dot(ypre_scr[...], wbh_ref[:, cols])
        y_att = _dot(opre_scr[...], wba_ref[:, cols])
        mixin_scr[:, cols] = (g_hg * y_hg + g_att * y_att).astype(jnp.bfloat16)
    for k in range(D_MODEL // DENSE_TN):
        cols = slice(k * DENSE_TN, (k + 1) * DENSE_TN)
        res_scr[:, cols] = ALPHA * h_scr[:, cols] + _dot(mixin_scr[...], wout_ref[:, cols])
    h_scr[...] = h_next
    hb_scr[...] = hb_next
    input_proj(hb_next, range(0, 4), cos_ref, sin_ref)
    out = _layer_norm(res_scr[...], ln1g_ref[...], ln1b_ref[...])
    out_ref[...] = out.reshape(BB, BLOCK, D_MODEL)
    input_proj(hb_next, range(4, n_proj), cos_ref, sin_ref)


def _ffn_kernel(h_ref, wa_ref, wu_ref, wo_ref, g_ref, b_ref, out_ref, res_scr):
    @pl.when(pl.program_id(0) == 0)
    def _():
        res_scr[...] = jnp.zeros_like(res_scr)

    h = h_ref[...]
    hb = h.astype(jnp.bfloat16)
    acc = jnp.zeros((FFN_TM, D_MODEL), jnp.float32)
    for n, (lo, hi) in enumerate(zip(FFN_SPLITS[:-1], FFN_SPLITS[1:])):
        sl = slice(lo, hi)
        a = _dot(hb, wa_ref[:, sl])
        u = _dot(hb, wu_ref[:, sl])
        hid = (a * _sigmoid(a) * u).astype(jnp.bfloat16)
        acc = acc + _dot(hid, wo_ref[sl, :])
        if n == 0:
            out_ref[...] = _layer_norm(res_scr[...], g_ref[...], b_ref[...])
    res_scr[...] = ALPHA * h + acc


def _const_spec(shape):
    nd = len(shape)
    return pl.BlockSpec(shape, lambda *_: (0,) * nd, pipeline_mode=pl.Buffered(1))


def kernel(x, meta_tokens, ln_emb_g, ln_emb_b, w_in, hg_lower_bounds, hg_norm_g, attn_sinks,
           w_branch_hg, w_branch_attn, w_out, ln1_g, ln1_b, w_ffn_in, w_ffn_out, ln2_g, ln2_b):
    B, S, D = x.shape
    assert D == D_MODEL and S % BLOCK == 0 and B % BB == 0 and (B * S) % FFN_TM == 0
    n_chunks = S // BLOCK
    f32, bf16 = jnp.float32, jnp.bfloat16
    row = lambda v: v.astype(f32).reshape(1, -1)

    lb = jnp.cumsum(jax.nn.softmax(hg_lower_bounds.astype(f32), axis=0), axis=0)[0].reshape(1, HG_W)
    normg = jnp.tile(hg_norm_g[0].astype(f32), HG_HEADS).reshape(1, HG_W)
    w0 = w_in[0].astype(bf16)
    o_att = 4 * HG_W
    o_gate = o_att + ATT_QW + 2 * ATT_KVW
    w_hg, w_att, w_gate = w0[:, :o_att], w0[:, o_att:o_gate], w0[:, o_gate:]
    half = HEAD_DIM // 2
    inv = ROPE_THETA ** (-jnp.arange(half, dtype=f32) / half)
    ang = jnp.arange(S + N_META, dtype=jnp.int32).astype(f32)[:, None] * inv[None, :]
    cos_t = jnp.tile(jnp.cos(ang), (1, 4))
    sin_t = jnp.tile(jnp.concatenate([-jnp.sin(ang), jnp.sin(ang)], axis=1), (1, 2))
    wcat_np, lv_np = _level_tables()
    wcat = jnp.asarray(wcat_np, bf16)
    lv = jnp.asarray(lv_np)
    ucat = jnp.asarray(np.tile(np.triu(np.ones((N_META, N_META), np.float32), 1), (1, 2)), bf16)
    bias = jnp.asarray(_attn_bias())
    cparams = functools.partial(pltpu.CompilerParams, vmem_limit_bytes=VMEM_LIMIT)

    state0, kkm, vvm = pl.pallas_call(
        _meta_kernel,
        out_shape=(jax.ShapeDtypeStruct((HG_HEADS, HG_K, HG_K), f32),
                   jax.ShapeDtypeStruct((ATT_KV_HEADS, BLOCK, 2 * HEAD_DIM), bf16),
                   jax.ShapeDtypeStruct((ATT_KV_HEADS, BLOCK, 2 * HEAD_DIM), bf16)),
        name="meta",
    )(meta_tokens.astype(f32), row(ln_emb_g), row(ln_emb_b),
      w_hg[:, HG_W:2 * HG_W], w_hg[:, 2 * HG_W:3 * HG_W],
      w_att[:, ATT_QW:ATT_QW + ATT_KVW], w_att[:, ATT_QW + ATT_KVW:],
      lb, cos_t[:N_META], sin_t[:N_META], ucat)

    m = BB * BLOCK
    h1 = pl.pallas_call(
        _mixer_kernel,
        grid=(B // BB, n_chunks),
        in_specs=[
            pl.BlockSpec((BB, BLOCK, D), lambda i, c: (i, jnp.minimum(c + 1, n_chunks - 1), 0)),
            pl.BlockSpec((BB, BLOCK, D), lambda i, c: (i, 0, 0)),
            pl.BlockSpec((BLOCK, 2 * HEAD_DIM), lambda i, c: (jnp.minimum(c + 1, n_chunks - 1), 0)),
            pl.BlockSpec((BLOCK, 2 * HEAD_DIM), lambda i, c: (jnp.minimum(c + 1, n_chunks - 1), 0)),
            pl.BlockSpec((BLOCK, 2 * HEAD_DIM), lambda i, c: (0, 0)),
            pl.BlockSpec((BLOCK, 2 * HEAD_DIM), lambda i, c: (0, 0)),
            _const_spec(bias.shape), _const_spec(lv.shape), _const_spec(wcat.shape),
            _const_spec((1, D)), _const_spec((1, D)),
            _const_spec(w_hg.shape), _const_spec(w_att.shape), _const_spec(w_gate.shape),
            _const_spec((1, HG_W)), _const_spec((1, HG_W)),
            pl.BlockSpec(memory_space=pltpu.SMEM),
            _const_spec((HG_W, D)), _const_spec((ATT_QW, D)), _const_spec((D, D)),
            _const_spec((1, D)), _const_spec((1, D)),
            _const_spec(state0.shape), _const_spec(kkm.shape), _const_spec(vvm.shape),
        ],
        out_specs=pl.BlockSpec((BB, BLOCK, D), lambda i, c: (i, c, 0)),
        out_shape=jax.ShapeDtypeStruct((B, S, D), f32),
        scratch_shapes=[
            pltpu.VMEM((BB, HG_HEADS, HG_K, HG_K), f32),
            pltpu.VMEM((BB, BLOCK, ATT_KVW), f32),
            pltpu.VMEM((BB, BLOCK, ATT_KVW), f32),
            pltpu.VMEM((m, D), f32),
            pltpu.VMEM((m, D), bf16),
            pltpu.VMEM((m, 4 * HG_W), f32),
            pltpu.VMEM((m, ATT_QW + 2 * ATT_KVW), f32),
            pltpu.VMEM((m, HG_W), bf16),
            pltpu.VMEM((m, ATT_QW), bf16),
            pltpu.VMEM((m, D), bf16),
            pltpu.VMEM((m, D), f32),
            pltpu.VMEM((m, HG_W), f32),
        ],
        compiler_params=cparams(dimension_semantics=("arbitrary", "arbitrary")),
        name="mixer",
    )(x.astype(f32), x.astype(f32), cos_t[N_META:], sin_t[N_META:], cos_t[N_META:], sin_t[N_META:],
      bias, lv, wcat,
      row(ln_emb_g), row(ln_emb_b),
      w_hg, w_att, w_gate, lb, normg, attn_sinks[0].astype(f32),
      w_branch_hg[0].astype(bf16), w_branch_attn[0].astype(bf16), w_out[0].astype(bf16),
      row(ln1_g[0]), row(ln1_b[0]), state0, kkm, vvm)

    wf = w_ffn_in[0].astype(bf16)
    n_tiles = B * S // FFN_TM
    out = pl.pallas_call(
        _ffn_kernel,
        grid=(n_tiles + 1,),
        in_specs=[
            pl.BlockSpec((FFN_TM, D), lambda i: (jnp.minimum(i, n_tiles - 1), 0)),
            _const_spec((D, D_FF)), _const_spec((D, D_FF)), _const_spec((D_FF, D)),
            _const_spec((1, D)), _const_spec((1, D)),
        ],
        out_specs=pl.BlockSpec((FFN_TM, D), lambda i: (jnp.maximum(i - 1, 0), 0)),
        out_shape=jax.ShapeDtypeStruct((B * S, D), f32),
        scratch_shapes=[pltpu.VMEM((FFN_TM, D), f32)],
        compiler_params=cparams(dimension_semantics=("arbitrary",)),
        name="ffn",
    )(h1.reshape(B * S, D), wf[:, :D_FF], wf[:, D_FF:], w_ffn_out[0].astype(bf16),
      row(ln2_g[0]), row(ln2_b[0]))
    return out.reshape(B, S, D)
```

```python
import functools

import numpy as np
import jax
import jax.numpy as jnp
from jax import lax
from jax.experimental import pallas as pl
from jax.experimental.pallas import tpu as pltpu

D_MODEL = 1024
N_META = 16
BLOCK = 128
HG_HEADS = 4
HG_K = 128
HG_W = HG_HEADS * HG_K
ATT_HEADS = 8
ATT_KV_HEADS = 2
HEAD_DIM = 64
ATT_QW = ATT_HEADS * HEAD_DIM
ATT_KVW = ATT_KV_HEADS * HEAD_DIM
D_FF = 2816
EPS = 1e-5
DEPTH = 1
ALPHA = (2.0 * DEPTH) ** 0.25
ROPE_THETA = 10000.0
NEG = -1e30
LOG2E = 1.4426950408889634

N_LEVELS = 7
DIAG_ID = N_LEVELS
MXU_LEVEL = N_LEVELS - 2
BB = 4
FFN_TM = 512
MXU_N = 256
SUBLANES = 8
FFN_SPLITS = (0, 6 * MXU_N, D_FF)
DENSE_TN = 512
VMEM_LIMIT = 56 * 1024 * 1024


def _level_tables():
    n = BLOCK
    t = np.arange(n)[:, None]
    j = np.arange(n)[None, :]
    blocks = [(j <= t)]
    lv = np.full((n, n), -1, np.int32)
    s = np.arange(n)[None, :]
    for l in range(N_LEVELS):
        c = n >> (l + 1)
        mid = (t // (2 * c)) * (2 * c) + c - 1
        second = (t % (2 * c)) >= c
        if l == MXU_LEVEL:
            blocks.append(np.where(second, (j > mid) & (j <= t), (j > t) & (j <= mid)))
        same = (t // (2 * c)) == (s // (2 * c))
        lv = np.where(same & second & ((s % (2 * c)) < c), l, lv)
    lv = np.where(t == s, DIAG_ID, lv).astype(np.int32)
    w = np.concatenate(blocks, axis=0).astype(np.float32)
    return np.concatenate([w, w], axis=1), lv


def _attn_bias():
    r = np.arange(BLOCK)[:, None]
    c = np.arange(2 * BLOCK)[None, :]
    cur = (c >= BLOCK) & ((c - BLOCK) <= r)
    prev = (c < BLOCK) & (c > r)
    b0 = np.where(cur, 0.0, NEG)
    b1 = np.where(cur | prev, 0.0, NEG)
    return np.stack([b0, b1]).astype(np.float32)


def _sigmoid(x):
    return 1.0 / (1.0 + jnp.exp(-x))


def _layer_norm(x, g, b):
    mu = jnp.mean(x, axis=-1, keepdims=True)
    xc = x - mu
    var = jnp.mean(xc * xc, axis=-1, keepdims=True)
    return xc * lax.rsqrt(var + EPS) * g + b


def _dot(a, b):
    return jnp.dot(a, b, preferred_element_type=jnp.float32)


def _dot_nt(a, b):
    return lax.dot_general(a, b, (((1,), (1,)), ((), ())), preferred_element_type=jnp.float32)


def _dot_tn(a, b):
    return lax.dot_general(a, b, (((0,), (0,)), ((), ())), preferred_element_type=jnp.float32)


def _split2(x):
    hi = x.astype(jnp.bfloat16)
    lo = (x - hi.astype(jnp.float32)).astype(jnp.bfloat16)
    return jnp.concatenate([hi, lo], axis=0)


def _rope(x, cos, sin):
    w = x.shape[1]
    lane = lax.broadcasted_iota(jnp.int32, x.shape, 1)
    partner = jnp.where((lane & (HEAD_DIM - 1)) < HEAD_DIM // 2,
                        pltpu.roll(x, w - HEAD_DIM // 2, 1), pltpu.roll(x, HEAD_DIM // 2, 1))
    reps = w // cos.shape[1]
    if reps > 1:
        cos = jnp.concatenate([cos] * reps, axis=1)
        sin = jnp.concatenate([sin] * reps, axis=1)
    return x * cos + partner * sin


def _lo_hi(a, group):
    lane = lax.broadcasted_iota(jnp.int32, a.shape, 1)
    swapped = pltpu.roll(a, HEAD_DIM, 1)
    zero = jnp.zeros_like(a)
    if group == 0:
        lo = jnp.where(lane < HEAD_DIM, a, zero)
        hi = jnp.where(lane >= HEAD_DIM, swapped, zero)
    else:
        lo = jnp.where(lane < HEAD_DIM, swapped, zero)
        hi = jnp.where(lane >= HEAD_DIM, a, zero)
    return jnp.concatenate([lo, hi], axis=0)


def _meta_kernel(meta_ref, g_ref, b_ref, wf_ref, wi_ref, wk_ref, wv_ref, lb_ref, cos_ref, sin_ref,
                 ucat_ref, state_ref, kkm_ref, vvm_ref):
    h = _layer_norm(meta_ref[...], g_ref[...], b_ref[...])
    hb = h.astype(jnp.bfloat16)
    lb = lb_ref[...]
    fg = lb + (1.0 - lb) * _sigmoid(_dot(hb, wf_ref[...]))
    log2_f = jnp.log2(fg)
    kk = 1.0 - fg
    vb = _dot(hb, wi_ref[...]).astype(jnp.bfloat16)
    suffix = _dot(ucat_ref[...], _split2(log2_f))
    ks = (kk * jnp.exp2(suffix)).astype(jnp.bfloat16)
    for hh in range(HG_HEADS):
        cols = slice(hh * HG_K, (hh + 1) * HG_K)
        state_ref[hh] = _dot_tn(vb[:, cols], ks[:, cols])
    k_rot = _rope(_dot(hb, wk_ref[...]), cos_ref[...], sin_ref[...])
    v = _dot(hb, wv_ref[...])
    pad = jnp.zeros((HEAD_DIM - N_META, 2 * HEAD_DIM), jnp.float32)
    for g in range(ATT_KV_HEADS):
        for src, dst in ((k_rot, kkm_ref), (v, vvm_ref)):
            lh = _lo_hi(src, g)
            dst[g] = jnp.concatenate([lh[:N_META], pad, lh[N_META:], pad], axis=0).astype(jnp.bfloat16)


def _hgrn2_pre(rows, phg_scr, kk_scr, wcat_ref):
    q = phg_scr[rows, 0 * HG_W:1 * HG_W]
    log2_f = phg_scr[rows, 1 * HG_W:2 * HG_W]
    pi = phg_scr[rows, 2 * HG_W:3 * HG_W]
    gate = phg_scr[rows, 3 * HG_W:4 * HG_W]
    kk = kk_scr[rows, :]
    e_mxu = _dot(wcat_ref[...], _split2(log2_f))
    b_cum = e_mxu[0:BLOCK]
    b_last = b_cum[BLOCK - 1:BLOCK, :]
    return dict(
        q=q, kk=kk, gate=gate, vb=pi.astype(jnp.bfloat16), b_cum=b_cum, e_mxu=e_mxu, log2_f=log2_f,
        q_in=(q * jnp.exp2(b_cum)).astype(jnp.bfloat16),
        k_st=(kk * jnp.exp2(b_last - b_cum)).astype(jnp.bfloat16),
        decay_last=jnp.exp2(b_last))


def _level_exponent(pre, l, cols):
    c = BLOCK >> (l + 1)
    if l == MXU_LEVEL:
        return pre["e_mxu"][BLOCK:2 * BLOCK, cols]
    b3 = pre["b_cum"][:, cols].reshape(BLOCK // (2 * c), 2 * c, HG_K)
    b_mid = b3[:, c - 1:c, :]
    if c >= SUBLANES:
        e3 = jnp.concatenate([b_mid - b3[:, :c], b3[:, c:] - b_mid], axis=1)
    else:
        e3 = -jnp.abs(b3 - b_mid)
    return e3.reshape(BLOCK, HG_K)


def _hgrn2_levels(pre, hh):
    cols = slice(hh * HG_K, (hh + 1) * HG_K)
    qh = pre["q"][:, cols]
    kh = pre["kk"][:, cols]
    ps = []
    for l in range(N_LEVELS):
        c = BLOCK >> (l + 1)
        if c == 1:
            f_t = jnp.exp2(pre["log2_f"][:, cols])
            ps.append(jnp.sum(qh * f_t * pltpu.roll(kh, 1, 0), axis=-1, keepdims=True))
            continue
        x = jnp.exp2(_level_exponent(pre, l, cols))
        if c >= SUBLANES:
            q3 = qh.reshape(BLOCK // (2 * c), 2 * c, HG_K)
            k3 = kh.reshape(BLOCK // (2 * c), 2 * c, HG_K)
            z = jnp.concatenate([k3[:, :c], q3[:, c:]], axis=1).reshape(BLOCK, HG_K)
            zb = (z * x).astype(jnp.bfloat16)
            ps.append(_dot_nt(zb, zb))
        else:
            ps.append(_dot_nt((qh * x).astype(jnp.bfloat16), (kh * x).astype(jnp.bfloat16)))
    ps.append(jnp.sum(qh * kh, axis=-1, keepdims=True))
    return ps


def _hgrn2_finish(pre, hh, ps, b, rows, state_scr, ypre_scr, lv_ref, normg_ref):
    cols = slice(hh * HG_K, (hh + 1) * HG_K)
    lv = lv_ref[...]
    vb = pre["vb"][:, cols]
    a = jnp.zeros((BLOCK, BLOCK), jnp.float32)
    for l, p in enumerate(ps):
        a = jnp.where(lv == l, p, a)
    st = state_scr[b, hh]
    o = _dot_nt(pre["q_in"][:, cols], st.astype(jnp.bfloat16)) + _dot(a.astype(jnp.bfloat16), vb)
    state_scr[b, hh] = st * pre["decay_last"][:, cols] + _dot_tn(vb, pre["k_st"][:, cols])
    r = lax.rsqrt(jnp.mean(o * o, axis=-1, keepdims=True) + EPS)
    ypre_scr[rows, cols] = (o * r * normg_ref[:, cols] * pre["gate"][:, cols]).astype(jnp.bfloat16)


def _attn_pre(b, rows, chunk, patt_scr, kband_scr, vband_scr, bias_ref):
    q = patt_scr[rows, 0:ATT_QW]
    k = patt_scr[rows, ATT_QW:ATT_QW + ATT_KVW]
    v = patt_scr[rows, ATT_QW + ATT_KVW:ATT_QW + 2 * ATT_KVW]
    kband = jnp.concatenate([kband_scr[b], k], axis=0)
    vband = jnp.concatenate([vband_scr[b], v], axis=0)
    kband_scr[b] = k
    vband_scr[b] = v
    return dict(qb=q.astype(jnp.bfloat16), kband=kband, vband=vband,
                bias=bias_ref[jnp.minimum(chunk, 1)])


def _attn_scores(pre, g, kkm_ref):
    qb = pre["qb"]
    qs = jnp.concatenate([qb[:, (2 * g) * 128:(2 * g + 1) * 128],
                          qb[:, (2 * g + 1) * 128:(2 * g + 2) * 128]], axis=0)
    s_band = _dot_nt(qs, _lo_hi(pre["kband"], g).astype(jnp.bfloat16))
    s_meta = _dot_nt(qs, kkm_ref[g])
    return s_band, s_meta


def _attn_softmax(pre, g, scores, sinks_ref):
    s_band, s_meta = scores
    bias = pre["bias"]
    lane = lax.broadcasted_iota(jnp.int32, (BLOCK, 2 * HEAD_DIM), 1)
    meta_mask = (lane < N_META, (lane >= HEAD_DIM) & (lane < HEAD_DIM + N_META))
    p_rows, pm_rows, rinv_rows = [], [], []
    for i in range(2):
        e_b, e_m, dinv = [], [], []
        for j in range(2):
            sink = sinks_ref[4 * g + 2 * i + j] * LOG2E
            sb = s_band[i * BLOCK:(i + 1) * BLOCK, j * 2 * BLOCK:(j + 1) * 2 * BLOCK] + bias
            sm = jnp.where(meta_mask[j], s_meta[i * BLOCK:(i + 1) * BLOCK, :], NEG)
            m = jnp.max(jnp.maximum(jnp.maximum(sb[:, :BLOCK], sb[:, BLOCK:]), sm), axis=-1, keepdims=True)
            m = jnp.maximum(m, sink)
            eb = jnp.exp2(sb - m)
            em = jnp.exp2(sm - m)
            d = (jnp.sum(eb[:, :BLOCK] + eb[:, BLOCK:] + em, axis=-1, keepdims=True)
                 + jnp.exp2(sink - m))
            e_b.append(eb.astype(jnp.bfloat16))
            e_m.append(em)
            dinv.append(1.0 / d)
        p_rows.append(jnp.concatenate(e_b, axis=1))
        pm_rows.append((e_m[0] + e_m[1]).astype(jnp.bfloat16))
        rinv_rows.append(jnp.where(lane < HEAD_DIM, dinv[0], dinv[1]))
    return (jnp.concatenate(p_rows, axis=0), jnp.concatenate(pm_rows, axis=0),
            jnp.concatenate(rinv_rows, axis=0))


def _attn_out(pre, g, probs, rows, opre_scr, vvm_ref):
    p, pm, rinv = probs
    o = _dot(p, _lo_hi(pre["vband"], g).astype(jnp.bfloat16)) + _dot(pm, vvm_ref[g])
    o = o * rinv
    opre_scr[rows, (2 * g) * 128:(2 * g + 1) * 128] = o[0:BLOCK].astype(jnp.bfloat16)
    opre_scr[rows, (2 * g + 1) * 128:(2 * g + 2) * 128] = o[BLOCK:2 * BLOCK].astype(jnp.bfloat16)


def _mixer_kernel(x_ref, x0_ref, cos_ref, sin_ref, cos0_ref, sin0_ref, bias_ref, lv_ref, wcat_ref, lng_ref, lnb_ref,
                  whg_ref, watt_ref, wgate_ref, lb_ref, normg_ref, sinks_ref,
                  wbh_ref, wba_ref, wout_ref, ln1g_ref, ln1b_ref, state0_ref, kkm_ref, vvm_ref,
                  out_ref,
                  state_scr, kband_scr, vband_scr, h_scr, hb_scr, phg_scr, patt_scr, ypre_scr, opre_scr,
                  mixin_scr, res_scr, kk_scr):
    chunk = pl.program_id(1)
    m = BB * BLOCK

    def embed(x_blocks):
        return _layer_norm(x_blocks.reshape(m, D_MODEL), lng_ref[...], lnb_ref[...])

    n_hg_blocks = 4 * HG_W // MXU_N
    n_proj = n_hg_blocks + (ATT_QW + 2 * ATT_KVW) // MXU_N

    def rope_rows(x, cos, sin):
        return jnp.concatenate([_rope(x[b * BLOCK:(b + 1) * BLOCK], cos, sin) for b in range(BB)], axis=0)

    def input_proj(hb, blocks, cos_r, sin_r):
        blocks_per_section = HG_W // MXU_N
        for k in blocks:
            if k < n_hg_blocks:
                cols = slice(k * MXU_N, (k + 1) * MXU_N)
                p = _dot(hb, whg_ref[:, cols])
                section = k // blocks_per_section
                if section == 1:
                    sub = slice(cols.start - HG_W, cols.stop - HG_W)
                    lb = lb_ref[:, sub]
                    fg = lb + (1.0 - lb) * _sigmoid(p)
                    kk_scr[:, sub] = 1.0 - fg
                    p = jnp.log2(fg)
                elif section in (0, 3):
                    p = p * _sigmoid(p)
                phg_scr[:, cols] = p
            else:
                cols = slice((k - n_hg_blocks) * MXU_N, (k - n_hg_blocks + 1) * MXU_N)
                p = _dot(hb, watt_ref[:, cols])
                cos, sin = cos_r[...], sin_r[...]
                if cols.stop <= ATT_QW:
                    p = rope_rows(p, cos, sin) * (HEAD_DIM ** -0.5 * LOG2E)
                else:
                    p = jnp.concatenate([rope_rows(p[:, :ATT_KVW], cos, sin), p[:, ATT_KVW:]], axis=1)
                patt_scr[:, cols] = p

    @pl.when(chunk == 0)
    def _():
        for b in range(BB):
            state_scr[b] = state0_ref[...]
        kband_scr[...] = jnp.zeros_like(kband_scr)
        vband_scr[...] = jnp.zeros_like(vband_scr)
        h0 = embed(x0_ref[...])
        h_scr[...] = h0
        hb_scr[...] = h0.astype(jnp.bfloat16)
        input_proj(hb_scr[...], range(n_proj), cos0_ref, sin0_ref)

    heads, groups = [], []
    for b in range(BB):
        rows = slice(b * BLOCK, (b + 1) * BLOCK)
        hg = _hgrn2_pre(rows, phg_scr, kk_scr, wcat_ref)
        at = _attn_pre(b, rows, chunk, patt_scr, kband_scr, vband_scr, bias_ref)
        heads.append([(hg, hh, b, rows) for hh in range(HG_HEADS)])
        groups.append([(at, g, rows) for g in range(ATT_KV_HEADS)])
    heads = [u for per_head in zip(*heads) for u in per_head]
    groups = [u for per_kv in zip(*groups) for u in per_kv]
    levels, scores, probs = {}, {}, {}

    def head_stage1(n):
        hg, hh, _, _ = heads[n]
        levels[n] = _hgrn2_levels(hg, hh)

    def head_stage2(n):
        hg, hh, b, rows = heads[n]
        _hgrn2_finish(hg, hh, levels.pop(n), b, rows, state_scr, ypre_scr, lv_ref, normg_ref)

    def group_stage1(n):
        at, g, _ = groups[n]
        scores[n] = _attn_scores(at, g, kkm_ref)

    def group_stage2(n):
        at, g, _ = groups[n]
        probs[n] = _attn_softmax(at, g, scores.pop(n), sinks_ref)

    def group_stage3(n):
        at, g, rows = groups[n]
        _attn_out(at, g, probs.pop(n), rows, opre_scr, vvm_ref)

    n_groups = len(groups)
    for n in range(n_groups):
        group_stage1(n)
        head_stage1(2 * n)
        if n > 0:
            head_stage2(2 * n - 1)
        group_stage2(n)
        head_stage1(2 * n + 1)
        head_stage2(2 * n)
        group_stage3(n)
    head_stage2(2 * n_groups - 1)

    h_next = embed(x_ref[...])
    hb_next = h_next.astype(jnp.bfloat16)
    for k in range(D_MODEL // DENSE_TN):
        cols = slice(k * DENSE_TN, (k + 1) * DENSE_TN)
        cols2 = slice(D_MODEL + k * DENSE_TN, D_MODEL + (k + 1) * DENSE_TN)
        g_hg = _sigmoid(_dot(hb_scr[...], wgate_ref[:, cols]))
        g_att = _sigmoid(_dot(hb_scr[...], wgate_ref[:, cols2]))
        y_hg = _dot(ypre_scr[...], wbh_ref[:, cols])
        y_att = _dot(opre_scr[...], wba_ref[:, cols])
        mixin_scr[:, cols] = (g_hg * y_hg + g_att * y_att).astype(jnp.bfloat16)
    for k in range(D_MODEL // DENSE_TN):
        cols = slice(k * DENSE_TN, (k + 1) * DENSE_TN)
        res_scr[:, cols] = ALPHA * h_scr[:, cols] + _dot(mixin_scr[...], wout_ref[:, cols])
    h_scr[...] = h_next
    hb_scr[...] = hb_next
    input_proj(hb_next, range(0, 4), cos_ref, sin_ref)
    out = _layer_norm(res_scr[...], ln1g_ref[...], ln1b_ref[...])
    out_ref[...] = out.reshape(BB, BLOCK, D_MODEL)
    input_proj(hb_next, range(4, n_proj), cos_ref, sin_ref)


def _ffn_kernel(h_ref, wa_ref, wu_ref, wo_ref, g_ref, b_ref, out_ref, res_scr):
    @pl.when(pl.program_id(0) == 0)
    def _():
        res_scr[...] = jnp.zeros_like(res_scr)

    h = h_ref[...]
    hb = h.astype(jnp.bfloat16)
    acc = jnp.zeros((FFN_TM, D_MODEL), jnp.float32)
    for n, (lo, hi) in enumerate(zip(FFN_SPLITS[:-1], FFN_SPLITS[1:])):
        sl = slice(lo, hi)
        a = _dot(hb, wa_ref[:, sl])
        u = _dot(hb, wu_ref[:, sl])
        hid = (a * _sigmoid(a) * u).astype(jnp.bfloat16)
        acc = acc + _dot(hid, wo_ref[sl, :])
        if n == 0:
            out_ref[...] = _layer_norm(res_scr[...], g_ref[...], b_ref[...])
    res_scr[...] = ALPHA * h + acc


def _const_spec(shape):
    nd = len(shape)
    return pl.BlockSpec(shape, lambda *_: (0,) * nd, pipeline_mode=pl.Buffered(1))


def kernel(x, meta_tokens, ln_emb_g, ln_emb_b, w_in, hg_lower_bounds, hg_norm_g, attn_sinks,
           w_branch_hg, w_branch_attn, w_out, ln1_g, ln1_b, w_ffn_in, w_ffn_out, ln2_g, ln2_b):
    B, S, D = x.shape
    assert D == D_MODEL and S % BLOCK == 0 and B % BB == 0 and (B * S) % FFN_TM == 0
    n_chunks = S // BLOCK
    f32, bf16 = jnp.float32, jnp.bfloat16
    row = lambda v: v.astype(f32).reshape(1, -1)

    lb = jnp.cumsum(jax.nn.softmax(hg_lower_bounds.astype(f32), axis=0), axis=0)[0].reshape(1, HG_W)
    normg = jnp.tile(hg_norm_g[0].astype(f32), HG_HEADS).reshape(1, HG_W)
    w0 = w_in[0].astype(bf16)
    o_att = 4 * HG_W
    o_gate = o_att + ATT_QW + 2 * ATT_KVW
    w_hg, w_att, w_gate = w0[:, :o_att], w0[:, o_att:o_gate], w0[:, o_gate:]
    half = HEAD_DIM // 2
    inv = ROPE_THETA ** (-jnp.arange(half, dtype=f32) / half)
    ang = jnp.arange(S + N_META, dtype=jnp.int32).astype(f32)[:, None] * inv[None, :]
    cos_t = jnp.tile(jnp.cos(ang), (1, 4))
    sin_t = jnp.tile(jnp.concatenate([-jnp.sin(ang), jnp.sin(ang)], axis=1), (1, 2))
    wcat_np, lv_np = _level_tables()
    wcat = jnp.asarray(wcat_np, bf16)
    lv = jnp.asarray(lv_np)
    ucat = jnp.asarray(np.tile(np.triu(np.ones((N_META, N_META), np.float32), 1), (1, 2)), bf16)
    bias = jnp.asarray(_attn_bias())
    cparams = functools.partial(pltpu.CompilerParams, vmem_limit_bytes=VMEM_LIMIT)

    state0, kkm, vvm = pl.pallas_call(
        _meta_kernel,
        out_shape=(jax.ShapeDtypeStruct((HG_HEADS, HG_K, HG_K), f32),
                   jax.ShapeDtypeStruct((ATT_KV_HEADS, BLOCK, 2 * HEAD_DIM), bf16),
                   jax.ShapeDtypeStruct((ATT_KV_HEADS, BLOCK, 2 * HEAD_DIM), bf16)),
        name="meta",
    )(meta_tokens.astype(f32), row(ln_emb_g), row(ln_emb_b),
      w_hg[:, HG_W:2 * HG_W], w_hg[:, 2 * HG_W:3 * HG_W],
      w_att[:, ATT_QW:ATT_QW + ATT_KVW], w_att[:, ATT_QW + ATT_KVW:],
      lb, cos_t[:N_META], sin_t[:N_META], ucat)

    m = BB * BLOCK
    h1 = pl.pallas_call(
        _mixer_kernel,
        grid=(B // BB, n_chunks),
        in_specs=[
            pl.BlockSpec((BB, BLOCK, D), lambda i, c: (i, jnp.minimum(c + 1, n_chunks - 1), 0)),
            pl.BlockSpec((BB, BLOCK, D), lambda i, c: (i, 0, 0)),
            pl.BlockSpec((BLOCK, 2 * HEAD_DIM), lambda i, c: (jnp.minimum(c + 1, n_chunks - 1), 0)),
            pl.BlockSpec((BLOCK, 2 * HEAD_DIM), lambda i, c: (jnp.minimum(c + 1, n_chunks - 1), 0)),
            pl.BlockSpec((BLOCK, 2 * HEAD_DIM), lambda i, c: (0, 0)),
            pl.BlockSpec((BLOCK, 2 * HEAD_DIM), lambda i, c: (0, 0)),
            _const_spec(bias.shape), _const_spec(lv.shape), _const_spec(wcat.shape),
            _const_spec((1, D)), _const_spec((1, D)),
            _const_spec(w_hg.shape), _const_spec(w_att.shape), _const_spec(w_gate.shape),
            _const_spec((1, HG_W)), _const_spec((1, HG_W)),
            pl.BlockSpec(memory_space=pltpu.SMEM),
            _const_spec((HG_W, D)), _const_spec((ATT_QW, D)), _const_spec((D, D)),
            _const_spec((1, D)), _const_spec((1, D)),
            _const_spec(state0.shape), _const_spec(kkm.shape), _const_spec(vvm.shape),
        ],
        out_specs=pl.BlockSpec((BB, BLOCK, D), lambda i, c: (i, c, 0)),
        out_shape=jax.ShapeDtypeStruct((B, S, D), f32),
        scratch_shapes=[
            pltpu.VMEM((BB, HG_HEADS, HG_K, HG_K), f32),
            pltpu.VMEM((BB, BLOCK, ATT_KVW), f32),
            pltpu.VMEM((BB, BLOCK, ATT_KVW), f32),
            pltpu.VMEM((m, D), f32),
            pltpu.VMEM((m, D), bf16),
            pltpu.VMEM((m, 4 * HG_W), f32),
            pltpu.VMEM((m, ATT_QW + 2 * ATT_KVW), f32),
            pltpu.VMEM((m, HG_W), bf16),
            pltpu.VMEM((m, ATT_QW), bf16),
            pltpu.VMEM((m, D), bf16),
            pltpu.VMEM((m, D), f32),
            pltpu.VMEM((m, HG_W), f32),
        ],
        compiler_params=cparams(dimension_semantics=("arbitrary", "arbitrary")),
        name="mixer",
    )(x.astype(f32), x.astype(f32), cos_t[N_META:], sin_t[N_META:], cos_t[N_META:], sin_t[N_META:],
      bias, lv, wcat,
      row(ln_emb_g), row(ln_emb_b),
      w_hg, w_att, w_gate, lb, normg, attn_sinks[0].astype(f32),
      w_branch_hg[0].astype(bf16), w_branch_attn[0].astype(bf16), w_out[0].astype(bf16),
      row(ln1_g[0]), row(ln1_b[0]), state0, kkm, vvm)

    wf = w_ffn_in[0].astype(bf16)
    n_tiles = B * S // FFN_TM
    out = pl.pallas_call(
        _ffn_kernel,
        grid=(n_tiles + 1,),
        in_specs=[
            pl.BlockSpec((FFN_TM, D), lambda i: (jnp.minimum(i, n_tiles - 1), 0)),
            _const_spec((D, D_FF)), _const_spec((D, D_FF)), _const_spec((D_FF, D)),
            _const_spec((1, D)), _const_spec((1, D)),
        ],
        out_specs=pl.BlockSpec((FFN_TM, D), lambda i: (jnp.maximum(i - 1, 0), 0)),
        out_shape=jax.ShapeDtypeStruct((B * S, D), f32),
        scratch_shapes=[pltpu.VMEM((FFN_TM, D), f32)],
        compiler_params=cparams(dimension_semantics=("arbitrary",)),
        name="ffn",
    )(h1.reshape(B * S, D), wf[:, :D_FF], wf[:, D_FF:], w_ffn_out[0].astype(bf16),
      row(ln2_g[0]), row(ln2_b[0]))
    return out.reshape(B, S, D)
```

```python
import functools

import numpy as np
import jax
import jax.numpy as jnp
from jax import lax
from jax.experimental import pallas as pl
from jax.experimental.pallas import tpu as pltpu

D_MODEL = 1024
N_META = 16
BLOCK = 128
HG_HEADS = 4
HG_K = 128
HG_W = HG_HEADS * HG_K
ATT_HEADS = 8
ATT_KV_HEADS = 2
HEAD_DIM = 64
ATT_QW = ATT_HEADS * HEAD_DIM
ATT_KVW = ATT_KV_HEADS * HEAD_DIM
D_FF = 2816
EPS = 1e-5
DEPTH = 1
ALPHA = (2.0 * DEPTH) ** 0.25
ROPE_THETA = 10000.0
NEG = -1e30
LOG2E = 1.4426950408889634

N_LEVELS = 7
DIAG_ID = N_LEVELS
MXU_LEVEL = N_LEVELS - 2
BB = 4
FFN_TM = 512
MXU_N = 256
SUBLANES = 8
FFN_SPLITS = (0, 6 * MXU_N, D_FF)
DENSE_TN = 512
VMEM_LIMIT = 56 * 1024 * 1024


def _level_tables():
    n = BLOCK
    t = np.arange(n)[:, None]
    j = np.arange(n)[None, :]
    blocks = [(j <= t)]
    lv = np.full((n, n), -1, np.int32)
    s = np.arange(n)[None, :]
    for l in range(N_LEVELS):
        c = n >> (l + 1)
        mid = (t // (2 * c)) * (2 * c) + c - 1
        second = (t % (2 * c)) >= c
        if l == MXU_LEVEL:
            blocks.append(np.where(second, (j > mid) & (j <= t), (j > t) & (j <= mid)))
        same = (t // (2 * c)) == (s // (2 * c))
        lv = np.where(same & second & ((s % (2 * c)) < c), l, lv)
    lv = np.where(t == s, DIAG_ID, lv).astype(np.int32)
    w = np.concatenate(blocks, axis=0).astype(np.float32)
    return np.concatenate([w, w], axis=1), lv


def _attn_bias():
    r = np.arange(BLOCK)[:, None]
    c = np.arange(2 * BLOCK)[None, :]
    cur = (c >= BLOCK) & ((c - BLOCK) <= r)
    prev = (c < BLOCK) & (c > r)
    b0 = np.where(cur, 0.0, NEG)
    b1 = np.where(cur | prev, 0.0, NEG)
    return np.stack([b0, b1]).astype(np.float32)


def _sigmoid(x):
    return 1.0 / (1.0 + jnp.exp(-x))


def _layer_norm(x, g, b):
    mu = jnp.mean(x, axis=-1, keepdims=True)
    xc = x - mu
    var = jnp.mean(xc * xc, axis=-1, keepdims=True)
    return xc * lax.rsqrt(var + EPS) * g + b


def _dot(a, b):
    return jnp.dot(a, b, preferred_element_type=jnp.float32)


def _dot_nt(a, b):
    return lax.dot_general(a, b, (((1,), (1,)), ((), ())), preferred_element_type=jnp.float32)


def _dot_tn(a, b):
    return lax.dot_general(a, b, (((0,), (0,)), ((), ())), preferred_element_type=jnp.float32)


def _split2(x):
    hi = x.astype(jnp.bfloat16)
    lo = (x - hi.astype(jnp.float32)).astype(jnp.bfloat16)
    return jnp.concatenate([hi, lo], axis=0)


def _rope(x, cos, sin):
    w = x.shape[1]
    lane = lax.broadcasted_iota(jnp.int32, x.shape, 1)
    partner = jnp.where((lane & (HEAD_DIM - 1)) < HEAD_DIM // 2,
                        pltpu.roll(x, w - HEAD_DIM // 2, 1), pltpu.roll(x, HEAD_DIM // 2, 1))
    reps = w // cos.shape[1]
    if reps > 1:
        cos = jnp.concatenate([cos] * reps, axis=1)
        sin = jnp.concatenate([sin] * reps, axis=1)
    return x * cos + partner * sin


def _lo_hi(a, group):
    lane = lax.broadcasted_iota(jnp.int32, a.shape, 1)
    swapped = pltpu.roll(a, HEAD_DIM, 1)
    zero = jnp.zeros_like(a)
    if group == 0:
        lo = jnp.where(lane < HEAD_DIM, a, zero)
        hi = jnp.where(lane >= HEAD_DIM, swapped, zero)
    else:
        lo = jnp.where(lane < HEAD_DIM, swapped, zero)
        hi = jnp.where(lane >= HEAD_DIM, a, zero)
    return jnp.concatenate([lo, hi], axis=0)


def _meta_kernel(meta_ref, g_ref, b_ref, wf_ref, wi_ref, wk_ref, wv_ref, lb_ref, cos_ref, sin_ref,
                 ucat_ref, state_ref, kkm_ref, vvm_ref):
    h = _layer_norm(meta_ref[...], g_ref[...], b_ref[...])
    hb = h.astype(jnp.bfloat16)
    lb = lb_ref[...]
    fg = lb + (1.0 - lb) * _sigmoid(_dot(hb, wf_ref[...]))
    log2_f = jnp.log2(fg)
    kk = 1.0 - fg
    vb = _dot(hb, wi_ref[...]).astype(jnp.bfloat16)
    suffix = _dot(ucat_ref[...], _split2(log2_f))
    ks = (kk * jnp.exp2(suffix)).astype(jnp.bfloat16)
    for hh in range(HG_HEADS):
        cols = slice(hh * HG_K, (hh + 1) * HG_K)
        state_ref[hh] = _dot_tn(vb[:, cols], ks[:, cols])
    k_rot = _rope(_dot(hb, wk_ref[...]), cos_ref[...], sin_ref[...])
    v = _dot(hb, wv_ref[...])
    pad = jnp.zeros((HEAD_DIM - N_META, 2 * HEAD_DIM), jnp.float32)
    for g in range(ATT_KV_HEADS):
        for src, dst in ((k_rot, kkm_ref), (v, vvm_ref)):
            lh = _lo_hi(src, g)
            dst[g] = jnp.concatenate([lh[:N_META], pad, lh[N_META:], pad], axis=0).astype(jnp.bfloat16)


def _hgrn2_pre(rows, phg_scr, kk_scr, wcat_ref):
    q = phg_scr[rows, 0 * HG_W:1 * HG_W]
    log2_f = phg_scr[rows, 1 * HG_W:2 * HG_W]
    pi = phg_scr[rows, 2 * HG_W:3 * HG_W]
    gate = phg_scr[rows, 3 * HG_W:4 * HG_W]
    kk = kk_scr[rows, :]
    e_mxu = _dot(wcat_ref[...], _split2(log2_f))
    b_cum = e_mxu[0:BLOCK]
    b_last = b_cum[BLOCK - 1:BLOCK, :]
    return dict(
        q=q, kk=kk, gate=gate, vb=pi.astype(jnp.bfloat16), b_cum=b_cum, e_mxu=e_mxu, log2_f=log2_f,
        q_in=(q * jnp.exp2(b_cum)).astype(jnp.bfloat16),
        k_st=(kk * jnp.exp2(b_last - b_cum)).astype(jnp.bfloat16),
        decay_last=jnp.exp2(b_last))


def _level_exponent(pre, l, cols):
    c = BLOCK >> (l + 1)
    if l == MXU_LEVEL:
        return pre["e_mxu"][BLOCK:2 * BLOCK, cols]
    b3 = pre["b_cum"][:, cols].reshape(BLOCK // (2 * c), 2 * c, HG_K)
    b_mid = b3[:, c - 1:c, :]
    if c >= SUBLANES:
        e3 = jnp.concatenate([b_mid - b3[:, :c], b3[:, c:] - b_mid], axis=1)
    else:
        e3 = -jnp.abs(b3 - b_mid)
    return e3.reshape(BLOCK, HG_K)


def _hgrn2_levels(pre, hh):
    cols = slice(hh * HG_K, (hh + 1) * HG_K)
    qh = pre["q"][:, cols]
    kh = pre["kk"][:, cols]
    ps = []
    for l in range(N_LEVELS):
        c = BLOCK >> (l + 1)
        if c == 1:
            f_t = jnp.exp2(pre["log2_f"][:, cols])
            ps.append(jnp.sum(qh * f_t * pltpu.roll(kh, 1, 0), axis=-1, keepdims=True))
            continue
        x = jnp.exp2(_level_exponent(pre, l, cols))
        if c >= SUBLANES:
            q3 = qh.reshape(BLOCK // (2 * c), 2 * c, HG_K)
            k3 = kh.reshape(BLOCK // (2 * c), 2 * c, HG_K)
            z = jnp.concatenate([k3[:, :c], q3[:, c:]], axis=1).reshape(BLOCK, HG_K)
            zb = (z * x).astype(jnp.bfloat16)
            ps.append(_dot_nt(zb, zb))
        else:
            ps.append(_dot_nt((qh * x).astype(jnp.bfloat16), (kh * x).astype(jnp.bfloat16)))
    ps.append(jnp.sum(qh * kh, axis=-1, keepdims=True))
    return ps


def _hgrn2_finish(pre, hh, ps, b, rows, state_scr, ypre_scr, lv_ref, normg_ref):
    cols = slice(hh * HG_K, (hh + 1) * HG_K)
    lv = lv_ref[...]
    vb = pre["vb"][:, cols]
    a = jnp.zeros((BLOCK, BLOCK), jnp.float32)
    for l, p in enumerate(ps):
        a = jnp.where(lv == l, p, a)
    st = state_scr[b, hh]
    o = _dot_nt(pre["q_in"][:, cols], st.astype(jnp.bfloat16)) + _dot(a.astype(jnp.bfloat16), vb)
    state_scr[b, hh] = st * pre["decay_last"][:, cols] + _dot_tn(vb, pre["k_st"][:, cols])
    r = lax.rsqrt(jnp.mean(o * o, axis=-1, keepdims=True) + EPS)
    ypre_scr[rows, cols] = (o * r * normg_ref[:, cols] * pre["gate"][:, cols]).astype(jnp.bfloat16)


def _attn_pre(b, rows, chunk, patt_scr, kband_scr, vband_scr, bias_ref):
    q = patt_scr[rows, 0:ATT_QW]
    k = patt_scr[rows, ATT_QW:ATT_QW + ATT_KVW]
    v = patt_scr[rows, ATT_QW + ATT_KVW:ATT_QW + 2 * ATT_KVW]
    kband = jnp.concatenate([kband_scr[b], k], axis=0)
    vband = jnp.concatenate([vband_scr[b], v], axis=0)
    kband_scr[b] = k
    vband_scr[b] = v
    return dict(qb=q.astype(jnp.bfloat16), kband=kband, vband=vband,
                bias=bias_ref[jnp.minimum(chunk, 1)])


def _attn_scores(pre, g, kkm_ref):
    qb = pre["qb"]
    qs = jnp.concatenate([qb[:, (2 * g) * 128:(2 * g + 1) * 128],
                          qb[:, (2 * g + 1) * 128:(2 * g + 2) * 128]], axis=0)
    s_band = _dot_nt(qs, _lo_hi(pre["kband"], g).astype(jnp.bfloat16))
    s_meta = _dot_nt(qs, kkm_ref[g])
    return s_band, s_meta


def _attn_softmax(pre, g, scores, sinks_ref):
    s_band, s_meta = scores
    bias = pre["bias"]
    lane = lax.broadcasted_iota(jnp.int32, (BLOCK, 2 * HEAD_DIM), 1)
    meta_mask = (lane < N_META, (lane >= HEAD_DIM) & (lane < HEAD_DIM + N_META))
    p_rows, pm_rows, rinv_rows = [], [], []
    for i in range(2):
        e_b, e_m, dinv = [], [], []
        for j in range(2):
            sink = sinks_ref[4 * g + 2 * i + j] * LOG2E
            sb = s_band[i * BLOCK:(i + 1) * BLOCK, j * 2 * BLOCK:(j + 1) * 2 * BLOCK] + bias
            sm = jnp.where(meta_mask[j], s_meta[i * BLOCK:(i + 1) * BLOCK, :], NEG)
            m = jnp.max(jnp.maximum(jnp.maximum(sb[:, :BLOCK], sb[:, BLOCK:]), sm), axis=-1, keepdims=True)
            m = jnp.maximum(m, sink)
            eb = jnp.exp2(sb - m)
            em = jnp.exp2(sm - m)
            d = (jnp.sum(eb[:, :BLOCK] + eb[:, BLOCK:] + em, axis=-1, keepdims=True)
                 + jnp.exp2(sink - m))
            e_b.append(eb.astype(jnp.bfloat16))
            e_m.append(em)
            dinv.append(1.0 / d)
        p_rows.append(jnp.concatenate(e_b, axis=1))
        pm_rows.append((e_m[0] + e_m[1]).astype(jnp.bfloat16))
        rinv_rows.append(jnp.where(lane < HEAD_DIM, dinv[0], dinv[1]))
    return (jnp.concatenate(p_rows, axis=0), jnp.concatenate(pm_rows, axis=0),
            jnp.concatenate(rinv_rows, axis=0))


def _attn_out(pre, g, probs, rows, opre_scr, vvm_ref):
    p, pm, rinv = probs
    o = _dot(p, _lo_hi(pre["vband"], g).astype(jnp.bfloat16)) + _dot(pm, vvm_ref[g])
    o = o * rinv
    opre_scr[rows, (2 * g) * 128:(2 * g + 1) * 128] = o[0:BLOCK].astype(jnp.bfloat16)
    opre_scr[rows, (2 * g + 1) * 128:(2 * g + 2) * 128] = o[BLOCK:2 * BLOCK].astype(jnp.bfloat16)


def _mixer_kernel(x_ref, x0_ref, cos_ref, sin_ref, cos0_ref, sin0_ref, bias_ref, lv_ref, wcat_ref, lng_ref, lnb_ref,
                  whg_ref, watt_ref, wgate_ref, lb_ref, normg_ref, sinks_ref,
                  wbh_ref, wba_ref, wout_ref, ln1g_ref, ln1b_ref, state0_ref, kkm_ref, vvm_ref,
                  out_ref,
                  state_scr, kband_scr, vband_scr, h_scr, hb_scr, phg_scr, patt_scr, ypre_scr, opre_scr,
                  mixin_scr, res_scr, kk_scr):
    chunk = pl.program_id(1)
    m = BB * BLOCK

    def embed(x_blocks):
        return _layer_norm(x_blocks.reshape(m, D_MODEL), lng_ref[...], lnb_ref[...])

    n_hg_blocks = 4 * HG_W // MXU_N
    n_proj = n_hg_blocks + (ATT_QW + 2 * ATT_KVW) // MXU_N

    def rope_rows(x, cos, sin):
        return jnp.concatenate([_rope(x[b * BLOCK:(b + 1) * BLOCK], cos, sin) for b in range(BB)], axis=0)

    def input_proj(hb, blocks, cos_r, sin_r):
        blocks_per_section = HG_W // MXU_N
        for k in blocks:
            if k < n_hg_blocks:
                cols = slice(k * MXU_N, (k + 1) * MXU_N)
                p = _dot(hb, whg_ref[:, cols])
                section = k // blocks_per_section
                if section == 1:
                    sub = slice(cols.start - HG_W, cols.stop - HG_W)
                    lb = lb_ref[:, sub]
                    fg = lb + (1.0 - lb) * _sigmoid(p)
                    kk_scr[:, sub] = 1.0 - fg
                    p = jnp.log2(fg)
                elif section in (0, 3):
                    p = p * _sigmoid(p)
                phg_scr[:, cols] = p
            else:
                cols = slice((k - n_hg_blocks) * MXU_N, (k - n_hg_blocks + 1) * MXU_N)
                p = _dot(hb, watt_ref[:, cols])
                cos, sin = cos_r[...], sin_r[...]
                if cols.stop <= ATT_QW:
                    p = rope_rows(p, cos, sin) * (HEAD_DIM ** -0.5 * LOG2E)
                else:
                    p = jnp.concatenate([rope_rows(p[:, :ATT_KVW], cos, sin), p[:, ATT_KVW:]], axis=1)
                patt_scr[:, cols] = p

    @pl.when(chunk == 0)
    def _():
        for b in range(BB):
            state_scr[b] = state0_ref[...]
        kband_scr[...] = jnp.zeros_like(kband_scr)
        vband_scr[...] = jnp.zeros_like(vband_scr)
        h0 = embed(x0_ref[...])
        h_scr[...] = h0
        hb_scr[...] = h0.astype(jnp.bfloat16)
        input_proj(hb_scr[...], range(n_proj), cos0_ref, sin0_ref)

    heads, groups = [], []
    for b in range(BB):
        rows = slice(b * BLOCK, (b + 1) * BLOCK)
        hg = _hgrn2_pre(rows, phg_scr, kk_scr, wcat_ref)
        at = _attn_pre(b, rows, chunk, patt_scr, kband_scr, vband_scr, bias_ref)
        heads.append([(hg, hh, b, rows) for hh in range(HG_HEADS)])
        groups.append([(at, g, rows) for g in range(ATT_KV_HEADS)])
    heads = [u for per_head in zip(*heads) for u in per_head]
    groups = [u for per_kv in zip(*groups) for u in per_kv]
    levels, scores, probs = {}, {}, {}

    def head_stage1(n):
        hg, hh, _, _ = heads[n]
        levels[n] = _hgrn2_levels(hg, hh)

    def head_stage2(n):
        hg, hh, b, rows = heads[n]
        _hgrn2_finish(hg, hh, levels.pop(n), b, rows, state_scr, ypre_scr, lv_ref, normg_ref)

    def group_stage1(n):
        at, g, _ = groups[n]
        scores[n] = _attn_scores(at, g, kkm_ref)

    def group_stage2(n):
        at, g, _ = groups[n]
        probs[n] = _attn_softmax(at, g, scores.pop(n), sinks_ref)

    def group_stage3(n):
        at, g, rows = groups[n]
        _attn_out(at, g, probs.pop(n), rows, opre_scr, vvm_ref)

    n_groups = len(groups)
    for n in range(n_groups):
        group_stage1(n)
        head_stage1(2 * n)
        if n > 0:
            head_stage2(2 * n - 1)
        group_stage2(n)
        head_stage1(2 * n + 1)
        head_stage2(2 * n)
        group_stage3(n)
    head_stage2(2 * n_groups - 1)

    h_next = embed(x_ref[...])
    hb_next = h_next.astype(jnp.bfloat16)
    for k in range(D_MODEL // DENSE_TN):
        cols = slice(k * DENSE_TN, (k + 1) * DENSE_TN)
        cols2 = slice(D_MODEL + k * DENSE_TN, D_MODEL + (k + 1) * DENSE_TN)
        g_hg = _sigmoid(_dot(hb_scr[...], wgate_ref[:, cols]))
        g_att = _sigmoid(_dot(hb_scr[...], wgate_ref[:, cols2]))
        y_hg = _dot(ypre_scr[...], wbh_ref[:, cols])
        y_att = _dot(opre_scr[...], wba_ref[:, cols])
        mixin_scr[:, cols] = (g_hg * y_hg + g_att * y_att).astype(jnp.bfloat16)
    for k in range(D_MODEL // DENSE_TN):
        cols = slice(k * DENSE_TN, (k + 1) * DENSE_TN)
        res_scr[:, cols] = ALPHA * h_scr[:, cols] + _dot(mixin_scr[...], wout_ref[:, cols])
    h_scr[...] = h_next
    hb_scr[...] = hb_next
    input_proj(hb_next, range(0, 4), cos_ref, sin_ref)
    out = _layer_norm(res_scr[...], ln1g_ref[...], ln1b_ref[...])
    out_ref[...] = out.reshape(BB, BLOCK, D_MODEL)
    input_proj(hb_next, range(4, n_proj), cos_ref, sin_ref)


def _ffn_kernel(h_ref, wa_ref, wu_ref, wo_ref, g_ref, b_ref, out_ref, res_scr):
    @pl.when(pl.program_id(0) == 0)
    def _():
        res_scr[...] = jnp.zeros_like(res_scr)

    h = h_ref[...]
    hb = h.astype(jnp.bfloat16)
    acc = jnp.zeros((FFN_TM, D_MODEL), jnp.float32)
    for n, (lo, hi) in enumerate(zip(FFN_SPLITS[:-1], FFN_SPLITS[1:])):
        sl = slice(lo, hi)
        a = _dot(hb, wa_ref[:, sl])
        u = _dot(hb, wu_ref[:, sl])
        hid = (a * _sigmoid(a) * u).astype(jnp.bfloat16)
        acc = acc + _dot(hid, wo_ref[sl, :])
        if n == 0:
            out = _layer_norm(res_scr[...], g_ref[...], b_ref[...])
            out_ref[...] = out
            folded = out.reshape(FFN_TM // SUBLANES, SUBLANES, D_MODEL).sum(axis=0)
            folded = sum(folded[:, t * BLOCK:(t + 1) * BLOCK] for t in range(D_MODEL // BLOCK))
            bits = pltpu.bitcast(folded, jnp.uint32)
            bits = lax.shift_right_logical(lax.shift_right_logical(bits, jnp.uint32(16)), jnp.uint32(16))
            zero = pltpu.bitcast(bits, jnp.float32)
            hb = (h + jnp.tile(zero, (FFN_TM // SUBLANES, D_MODEL // BLOCK))).astype(jnp.bfloat16)
    res_scr[...] = ALPHA * h + acc


def _const_spec(shape):
    nd = len(shape)
    return pl.BlockSpec(shape, lambda *_: (0,) * nd, pipeline_mode=pl.Buffered(1))


def kernel(x, meta_tokens, ln_emb_g, ln_emb_b, w_in, hg_lower_bounds, hg_norm_g, attn_sinks,
           w_branch_hg, w_branch_attn, w_out, ln1_g, ln1_b, w_ffn_in, w_ffn_out, ln2_g, ln2_b):
    B, S, D = x.shape
    assert D == D_MODEL and S % BLOCK == 0 and B % BB == 0 and (B * S) % FFN_TM == 0
    n_chunks = S // BLOCK
    f32, bf16 = jnp.float32, jnp.bfloat16
    row = lambda v: v.astype(f32).reshape(1, -1)

    lb = jnp.cumsum(jax.nn.softmax(hg_lower_bounds.astype(f32), axis=0), axis=0)[0].reshape(1, HG_W)
    normg = jnp.tile(hg_norm_g[0].astype(f32), HG_HEADS).reshape(1, HG_W)
    w0 = w_in[0].astype(bf16)
    o_att = 4 * HG_W
    o_gate = o_att + ATT_QW + 2 * ATT_KVW
    w_hg, w_att, w_gate = w0[:, :o_att], w0[:, o_att:o_gate], w0[:, o_gate:]
    half = HEAD_DIM // 2
    inv = ROPE_THETA ** (-jnp.arange(half, dtype=f32) / half)
    ang = jnp.arange(S + N_META, dtype=jnp.int32).astype(f32)[:, None] * inv[None, :]
    cos_t = jnp.tile(jnp.cos(ang), (1, 4))
    sin_t = jnp.tile(jnp.concatenate([-jnp.sin(ang), jnp.sin(ang)], axis=1), (1, 2))
    wcat_np, lv_np = _level_tables()
    wcat = jnp.asarray(wcat_np, bf16)
    lv = jnp.asarray(lv_np)
    ucat = jnp.asarray(np.tile(np.triu(np.ones((N_META, N_META), np.float32), 1), (1, 2)), bf16)
    bias = jnp.asarray(_attn_bias())
    cparams = functools.partial(pltpu.CompilerParams, vmem_limit_bytes=VMEM_LIMIT)

    state0, kkm, vvm = pl.pallas_call(
        _meta_kernel,
        out_shape=(jax.ShapeDtypeStruct((HG_HEADS, HG_K, HG_K), f32),
                   jax.ShapeDtypeStruct((ATT_KV_HEADS, BLOCK, 2 * HEAD_DIM), bf16),
                   jax.ShapeDtypeStruct((ATT_KV_HEADS, BLOCK, 2 * HEAD_DIM), bf16)),
        name="meta",
    )(meta_tokens.astype(f32), row(ln_emb_g), row(ln_emb_b),
      w_hg[:, HG_W:2 * HG_W], w_hg[:, 2 * HG_W:3 * HG_W],
      w_att[:, ATT_QW:ATT_QW + ATT_KVW], w_att[:, ATT_QW + ATT_KVW:],
      lb, cos_t[:N_META], sin_t[:N_META], ucat)

    m = BB * BLOCK
    h1 = pl.pallas_call(
        _mixer_kernel,
        grid=(B // BB, n_chunks),
        in_specs=[
            pl.BlockSpec((BB, BLOCK, D), lambda i, c: (i, jnp.minimum(c + 1, n_chunks - 1), 0)),
            pl.BlockSpec((BB, BLOCK, D), lambda i, c: (i, 0, 0)),
            pl.BlockSpec((BLOCK, 2 * HEAD_DIM), lambda i, c: (jnp.minimum(c + 1, n_chunks - 1), 0)),
            pl.BlockSpec((BLOCK, 2 * HEAD_DIM), lambda i, c: (jnp.minimum(c + 1, n_chunks - 1), 0)),
            pl.BlockSpec((BLOCK, 2 * HEAD_DIM), lambda i, c: (0, 0)),
            pl.BlockSpec((BLOCK, 2 * HEAD_DIM), lambda i, c: (0, 0)),
            _const_spec(bias.shape), _const_spec(lv.shape), _const_spec(wcat.shape),
            _const_spec((1, D)), _const_spec((1, D)),
            _const_spec(w_hg.shape), _const_spec(w_att.shape), _const_spec(w_gate.shape),
            _const_spec((1, HG_W)), _const_spec((1, HG_W)),
            pl.BlockSpec(memory_space=pltpu.SMEM),
            _const_spec((HG_W, D)), _const_spec((ATT_QW, D)), _const_spec((D, D)),
            _const_spec((1, D)), _const_spec((1, D)),
            _const_spec(state0.shape), _const_spec(kkm.shape), _const_spec(vvm.shape),
        ],
        out_specs=pl.BlockSpec((BB, BLOCK, D), lambda i, c: (i, c, 0)),
        out_shape=jax.ShapeDtypeStruct((B, S, D), f32),
        scratch_shapes=[
            pltpu.VMEM((BB, HG_HEADS, HG_K, HG_K), f32),
            pltpu.VMEM((BB, BLOCK, ATT_KVW), f32),
            pltpu.VMEM((BB, BLOCK, ATT_KVW), f32),
            pltpu.VMEM((m, D), f32),
            pltpu.VMEM((m, D), bf16),
            pltpu.VMEM((m, 4 * HG_W), f32),
            pltpu.VMEM((m, ATT_QW + 2 * ATT_KVW), f32),
            pltpu.VMEM((m, HG_W), bf16),
            pltpu.VMEM((m, ATT_QW), bf16),
            pltpu.VMEM((m, D), bf16),
            pltpu.VMEM((m, D), f32),
            pltpu.VMEM((m, HG_W), f32),
        ],
        compiler_params=cparams(dimension_semantics=("arbitrary", "arbitrary")),
        name="mixer",
    )(x.astype(f32), x.astype(f32), cos_t[N_META:], sin_t[N_META:], cos_t[N_META:], sin_t[N_META:],
      bias, lv, wcat,
      row(ln_emb_g), row(ln_emb_b),
      w_hg, w_att, w_gate, lb, normg, attn_sinks[0].astype(f32),
      w_branch_hg[0].astype(bf16), w_branch_attn[0].astype(bf16), w_out[0].astype(bf16),
      row(ln1_g[0]), row(ln1_b[0]), state0, kkm, vvm)

    wf = w_ffn_in[0].astype(bf16)
    n_tiles = B * S // FFN_TM
    out = pl.pallas_call(
        _ffn_kernel,
        grid=(n_tiles + 1,),
        in_specs=[
            pl.BlockSpec((FFN_TM, D), lambda i: (jnp.minimum(i, n_tiles - 1), 0)),
            _const_spec((D, D_FF)), _const_spec((D, D_FF)), _const_spec((D_FF, D)),
            _const_spec((1, D)), _const_spec((1, D)),
        ],
        out_specs=pl.BlockSpec((FFN_TM, D), lambda i: (jnp.maximum(i - 1, 0), 0)),
        out_shape=jax.ShapeDtypeStruct((B * S, D), f32),
        scratch_shapes=[pltpu.VMEM((FFN_TM, D), f32)],
        compiler_params=cparams(dimension_semantics=("arbitrary",)),
        name="ffn",
    )(h1.reshape(B * S, D), wf[:, :D_FF], wf[:, D_FF:], w_ffn_out[0].astype(bf16),
      row(ln2_g[0]), row(ln2_b[0]))
    return out.reshape(B, S, D)
```
